```python
import math
import jax, jax.numpy as jnp
from jax import lax
import numpy as np


D_MODEL = 1024
BATCH = 8
SEQ = 8192
DEPTH = 2
DEC_BATCH = 1
DEC_SEQ = 16384
PAST_LEN = 128

GRID_W = 64
NA_HEADS = 8
NA_HEAD_DIM = 64
NA_ROWS = 8
NA_COLS = 16
NA_WIDTH = NA_HEADS * NA_HEAD_DIM
MLA_HEADS = 4
MLA_Q_LORA = 384
MLA_KV_LORA = 256
MLA_NOPE = 128
MLA_ROPE = 64
MLA_V = 128
MLA_QK = MLA_NOPE + MLA_ROPE
MLA_WIDTH = MLA_HEADS * MLA_V
ROPE_THETA = 10000.0
Q_BLOCK = 128
MIX_WIDTH = NA_WIDTH + MLA_WIDTH
IN_WIDTH = 3 * NA_WIDTH + MLA_Q_LORA + MLA_KV_LORA + MLA_ROPE
D_FF = 2816
N_EXPERTS = 8
TOP_K = 2
D_FF_EXPERT = 2816
N_DENSE = (DEPTH + 1) // 2
N_MOE = DEPTH // 2
EPS = 1e-6

kernel_name = 'hymba_natten_mla_moe_encoder'


def rms_norm(x, g):
    xf = x.astype(jnp.float32)
    y = xf * lax.rsqrt(jnp.mean(xf * xf, axis=-1, keepdims=True) + EPS)
    return (y * g.astype(jnp.float32)).astype(x.dtype)


def rotary(x):
    S, R = x.shape[1], x.shape[-1]
    inv = ROPE_THETA ** (-jnp.arange(0, R, 2, dtype=jnp.float32) / R)
    ang = jnp.arange(S, dtype=jnp.float32)[:, None] * inv[None, :]
    cos = jnp.cos(ang)[None, :, None, :]
    sin = jnp.sin(ang)[None, :, None, :]
    xf = x.astype(jnp.float32)
    x1, x2 = xf[..., : R // 2], xf[..., R // 2:]
    return jnp.concatenate([x1 * cos - x2 * sin, x2 * cos + x1 * sin], axis=-1).astype(x.dtype)


def neighbourhood_attention(q, k, v, rpb):
    B, S, H, Dh = q.shape
    rows = S // GRID_W
    kr = min(NA_ROWS, rows)
    qg = q.reshape(B, rows, GRID_W, H, Dh)
    kg = k.reshape(B, rows, GRID_W, H, Dh)
    vg = v.reshape(B, rows, GRID_W, H, Dh)
    col = jnp.arange(GRID_W)
    col_start = jnp.clip(col - NA_COLS // 2, 0, GRID_W - NA_COLS)
    col_idx = col_start[:, None] + jnp.arange(NA_COLS)[None, :]
    dc = col_idx - col[:, None] + (NA_COLS - 1)
    scale = Dh ** -0.5

    def one_row(r):
        rs = jnp.clip(r - kr // 2, 0, rows - kr)
        k_rows = lax.dynamic_slice_in_dim(kg, rs, kr, axis=1)
        v_rows = lax.dynamic_slice_in_dim(vg, rs, kr, axis=1)
        k_nb = k_rows[:, :, col_idx]
        v_nb = v_rows[:, :, col_idx]
        q_row = lax.dynamic_index_in_dim(qg, r, axis=1, keepdims=False)
        s = jnp.einsum('bqhd,brqjhd->bhqrj', q_row, k_nb).astype(jnp.float32) * scale
        dr = rs + jnp.arange(kr) - r + (NA_ROWS - 1)
        bias = rpb[:, dr[:, None, None], dc[None, :, :]]
        s = s + jnp.transpose(bias, (0, 2, 1, 3))[None].astype(jnp.float32)
        p = jax.nn.softmax(s.reshape(B, H, GRID_W, kr * NA_COLS), axis=-1)
        p = p.reshape(B, H, GRID_W, kr, NA_COLS).astype(v.dtype)
        return jnp.einsum('bhqrj,brqjhd->bqhd', p, v_nb)

    out = lax.map(one_row, jnp.arange(rows))
    return jnp.transpose(out, (1, 0, 2, 3, 4)).reshape(B, S, H * Dh)


def dense_attention(q, k, v):
    B, S, H, Dq = q.shape
    Dv = v.shape[-1]
    nb = S // Q_BLOCK
    qb = jnp.transpose(q.reshape(B, nb, Q_BLOCK, H, Dq), (1, 0, 2, 3, 4))
    scale = Dq ** -0.5

    def one_block(qi):
        s = jnp.einsum('bqhd,bkhd->bhqk', qi, k).astype(jnp.float32) * scale
        p = jax.nn.softmax(s, axis=-1).astype(v.dtype)
        return jnp.einsum('bhqk,bkhd->bqhd', p, v)

    out = lax.map(one_block, qb)
    return jnp.transpose(out, (1, 0, 2, 3, 4)).reshape(B, S, H * Dv)


def token_mixer(h, w_in, na_q_norm, na_k_norm, na_rpb, mla_q_a_norm, w_uq, mla_kv_a_norm, w_ukv,
                mla_q_norm, mla_k_norm, na_out_norm, mla_out_norm, w_out):
    B, S, _ = h.shape
    proj = h @ w_in
    o1 = NA_WIDTH
    o2 = 2 * NA_WIDTH
    o3 = 3 * NA_WIDTH
    o4 = o3 + MLA_Q_LORA
    o5 = o4 + MLA_KV_LORA
    q_na, k_na, v_na, c_q, c_kv, k_pe = jnp.split(proj, [o1, o2, o3, o4, o5], axis=-1)
    q_na = rms_norm(q_na.reshape(B, S, NA_HEADS, NA_HEAD_DIM), na_q_norm)
    k_na = rms_norm(k_na.reshape(B, S, NA_HEADS, NA_HEAD_DIM), na_k_norm)
    v_na = v_na.reshape(B, S, NA_HEADS, NA_HEAD_DIM)
    out_a = neighbourhood_attention(q_na, k_na, v_na, na_rpb)
    q = (rms_norm(c_q, mla_q_a_norm) @ w_uq).reshape(B, S, MLA_HEADS, MLA_QK)
    kv = (rms_norm(c_kv, mla_kv_a_norm) @ w_ukv).reshape(B, S, MLA_HEADS, MLA_NOPE + MLA_V)
    k_nope, v_mla = kv[..., :MLA_NOPE], kv[..., MLA_NOPE:]
    k_pe = jnp.broadcast_to(k_pe[:, :, None, :], (B, S, MLA_HEADS, MLA_ROPE))
    k = jnp.concatenate([k_nope, k_pe], axis=-1)
    q = rms_norm(q, mla_q_norm)
    k = rms_norm(k, mla_k_norm)
    q = jnp.concatenate([q[..., :MLA_NOPE], rotary(q[..., MLA_NOPE:])], axis=-1)
    k = jnp.concatenate([k[..., :MLA_NOPE], rotary(k[..., MLA_NOPE:])], axis=-1)
    out_b = dense_attention(q, k, v_mla)
    merged = jnp.concatenate([rms_norm(out_a, na_out_norm), rms_norm(out_b, mla_out_norm)], axis=-1)
    return merged @ w_out


def swiglu(x, wg, wu, wd):
    return (jax.nn.silu(x @ wg) * (x @ wu)) @ wd


def moe_swiglu(x, w_router, wg, wu, wd):
    B, S, D = x.shape
    xt = x.reshape(B * S, D)
    logits = (xt @ w_router).astype(jnp.float32)
    top_val, top_idx = lax.top_k(logits, TOP_K)
    gates = jax.nn.softmax(top_val, axis=-1)
    gate_full = jnp.sum(jax.nn.one_hot(top_idx, N_EXPERTS, dtype=jnp.float32) * gates[..., None], axis=1)
    out = jnp.zeros_like(xt)
    for e in range(N_EXPERTS):
        out = out + gate_full[:, e:e + 1].astype(x.dtype) * swiglu(xt, wg[e], wu[e], wd[e])
    return out.reshape(B, S, D)


def trunk(x, norm_mix, norm_ffn, w_in, na_q_norm, na_k_norm, na_rpb, mla_q_a_norm, w_uq,
          mla_kv_a_norm, w_ukv, mla_q_norm, mla_k_norm, na_out_norm, mla_out_norm, w_out,
          ffn_w_gate, ffn_w_up, ffn_w_down, moe_router, moe_w_gate, moe_w_up, moe_w_down):
    for l in range(DEPTH):
        h = rms_norm(x, norm_mix[l])
        x = x + token_mixer(h, w_in[l], na_q_norm[l], na_k_norm[l], na_rpb[l], mla_q_a_norm[l], w_uq[l],
                            mla_kv_a_norm[l], w_ukv[l], mla_q_norm[l], mla_k_norm[l],
                            na_out_norm[l], mla_out_norm[l], w_out[l])
        h = rms_norm(x, norm_ffn[l])
        i = l // 2
        if l % 2 == 0:
            x = x + swiglu(h, ffn_w_gate[i], ffn_w_up[i], ffn_w_down[i])
        else:
            x = x + moe_swiglu(h, moe_router[i], moe_w_gate[i], moe_w_up[i], moe_w_down[i])
    return x


def setup_inputs(seed: int = 0) -> dict:
    key = jax.random.key(seed)
    ks = jax.random.split(key, 32)

    def nrm(k, shape, scale):
        return jax.random.normal(k, shape, dtype=jnp.float32) * scale

    def gain(k, shape):
        return 1.0 + 0.02 * jax.random.normal(k, shape, dtype=jnp.float32)

    return {
        'x_prompt': nrm(ks[0], (BATCH, SEQ, D_MODEL), 1.0),
        'x_sample': nrm(ks[1], (DEC_BATCH, DEC_SEQ, D_MODEL), 1.0),
        'norm_mix': gain(ks[2], (DEPTH, D_MODEL)),
        'norm_ffn': gain(ks[3], (DEPTH, D_MODEL)),
        'w_in': nrm(ks[4], (DEPTH, D_MODEL, IN_WIDTH), D_MODEL ** -0.5),
        'na_q_norm': gain(ks[5], (DEPTH, NA_HEAD_DIM)),
        'na_k_norm': gain(ks[6], (DEPTH, NA_HEAD_DIM)),
        'na_rpb': nrm(ks[7], (DEPTH, NA_HEADS, 2 * NA_ROWS - 1, 2 * NA_COLS - 1), 0.1),
        'mla_q_a_norm': gain(ks[8], (DEPTH, MLA_Q_LORA)),
        'w_uq': nrm(ks[9], (DEPTH, MLA_Q_LORA, MLA_HEADS * MLA_QK), MLA_Q_LORA ** -0.5),
        'mla_kv_a_norm': gain(ks[10], (DEPTH, MLA_KV_LORA)),
        'w_ukv': nrm(ks[11], (DEPTH, MLA_KV_LORA, MLA_HEADS * (MLA_NOPE + MLA_V)), MLA_KV_LORA ** -0.5),
        'mla_q_norm': gain(ks[12], (DEPTH, MLA_QK)),
        'mla_k_norm': gain(ks[13], (DEPTH, MLA_QK)),
        'na_out_norm': gain(ks[14], (DEPTH, NA_WIDTH)),
        'mla_out_norm': gain(ks[15], (DEPTH, MLA_WIDTH)),
        'w_out': nrm(ks[16], (DEPTH, MIX_WIDTH, D_MODEL), MIX_WIDTH ** -0.5),
        'ffn_w_gate': nrm(ks[17], (N_DENSE, D_MODEL, D_FF), D_MODEL ** -0.5),
        'ffn_w_up': nrm(ks[18], (N_DENSE, D_MODEL, D_FF), D_MODEL ** -0.5),
        'ffn_w_down': nrm(ks[19], (N_DENSE, D_FF, D_MODEL), D_FF ** -0.5),
        'moe_router': nrm(ks[20], (N_MOE, D_MODEL, N_EXPERTS), D_MODEL ** -0.5),
        'moe_w_gate': nrm(ks[21], (N_MOE, N_EXPERTS, D_MODEL, D_FF_EXPERT), D_MODEL ** -0.5),
        'moe_w_up': nrm(ks[22], (N_MOE, N_EXPERTS, D_MODEL, D_FF_EXPERT), D_MODEL ** -0.5),
        'moe_w_down': nrm(ks[23], (N_MOE, N_EXPERTS, D_FF_EXPERT, D_MODEL), D_FF_EXPERT ** -0.5),
    }


def reference(x_prompt, x_sample, norm_mix, norm_ffn, w_in, na_q_norm, na_k_norm, na_rpb, mla_q_a_norm,
              w_uq, mla_kv_a_norm, w_ukv, mla_q_norm, mla_k_norm, na_out_norm, mla_out_norm, w_out,
              ffn_w_gate, ffn_w_up, ffn_w_down, moe_router, moe_w_gate, moe_w_up, moe_w_down):
    y_prompt = trunk(x_prompt, norm_mix, norm_ffn, w_in, na_q_norm, na_k_norm, na_rpb, mla_q_a_norm, w_uq,
                     mla_kv_a_norm, w_ukv, mla_q_norm, mla_k_norm, na_out_norm, mla_out_norm, w_out,
                     ffn_w_gate, ffn_w_up, ffn_w_down, moe_router, moe_w_gate, moe_w_up, moe_w_down)
    y_sample = trunk(x_sample, norm_mix, norm_ffn, w_in, na_q_norm, na_k_norm, na_rpb, mla_q_a_norm, w_uq,
                     mla_kv_a_norm, w_ukv, mla_q_norm, mla_k_norm, na_out_norm, mla_out_norm, w_out,
                     ffn_w_gate, ffn_w_up, ffn_w_down, moe_router, moe_w_gate, moe_w_up, moe_w_down)
    return (y_prompt, y_sample)
```

```python
import functools
import math

import jax
import jax.numpy as jnp
from jax import lax
from jax.experimental import pallas as pl
from jax.experimental.pallas import tpu as pltpu

F32 = jnp.float32
BF16 = jnp.bfloat16

D_MODEL = 1024
GRID_W = 64
NA_HEADS = 8
NA_HEAD_DIM = 64
NA_ROWS = 8
NA_COLS = 16
NA_WIDTH = NA_HEADS * NA_HEAD_DIM
MLA_HEADS = 4
MLA_Q_LORA = 384
MLA_KV_LORA = 256
MLA_NOPE = 128
MLA_ROPE = 64
MLA_V = 128
MLA_QK = MLA_NOPE + MLA_ROPE
MLA_WIDTH = MLA_HEADS * MLA_V
MLA_QK_PAD = 256
ROPE_THETA = 10000.0
N_EXPERTS = 8
TOP_K = 2
EPS = 1e-6
NEG_INF = -1e30

LANES = 128
TOKEN_TILE = 512
NA_BLOCK_ROWS = 8
NA_WINDOW_ROWS = 16
FLASH_TQ = 512
FLASH_TK = 512
FFN_CHUNK = 1408
VMEM_LIMIT = 56 * 1024 * 1024


def _cparams(sem):
    return pltpu.CompilerParams(dimension_semantics=sem, vmem_limit_bytes=VMEM_LIMIT)


def _const_spec(shape):
    nd = len(shape)
    return pl.BlockSpec(shape, lambda *_: (0,) * nd, pipeline_mode=pl.Buffered(1))


def _rms(xf, g):
    y = xf * lax.rsqrt(jnp.mean(xf * xf, axis=-1, keepdims=True) + EPS)
    return y * g


def _split_bf16(v):
    hi = v.astype(BF16)
    lo = (v - hi.astype(F32)).astype(BF16)
    return hi, lo


def _proj_kernel(pos_ref, x_ref, gmix_ref, win_ref, gsum_ref, gqna_ref, gkna_ref, gqa_ref, wuq_ref,
                 gkva_ref, wukv_ref, gq_ref, gk_ref, cos_ref, sin_ref,
                 qna_ref, kna_ref, vna_ref, qm_ref, km_ref, vm_ref):
    del pos_ref
    h = _rms(x_ref[...], gmix_ref[...]).astype(BF16)

    def proj(lo, hi):
        return jnp.dot(h, win_ref[:, lo:hi], preferred_element_type=F32)

    gsum = gsum_ref[...]

    def head_norm(v, gain):
        hi, lo = _split_bf16(v * v)
        ss = jnp.dot(hi, gsum, preferred_element_type=F32) + jnp.dot(lo, gsum, preferred_element_type=F32)
        return (v * lax.rsqrt(ss * (1.0 / NA_HEAD_DIM) + EPS)) * gain

    o1, o2, o3 = NA_WIDTH, 2 * NA_WIDTH, 3 * NA_WIDTH
    o4 = o3 + MLA_Q_LORA
    o5 = o4 + MLA_KV_LORA
    qna_ref[...] = head_norm(proj(0, o1), gqna_ref[...]).astype(BF16)
    kna_ref[...] = head_norm(proj(o1, o2), gkna_ref[...]).astype(BF16)
    vna_ref[...] = proj(o2, o3).astype(BF16)

    cos = cos_ref[...]
    sin = sin_ref[...]

    def rope(u):
        return u * cos + pltpu.roll(u, 64, 1) * sin

    cq = _rms(proj(o3, o4), gqa_ref[...]).astype(BF16)
    q_all = jnp.dot(cq, wuq_ref[...], preferred_element_type=F32)
    gq = gq_ref[...]
    for hd in range(MLA_HEADS):
        qh = q_all[:, hd * MLA_QK_PAD:(hd + 1) * MLA_QK_PAD]
        r = lax.rsqrt(jnp.sum(qh * qh, axis=-1, keepdims=True) * (1.0 / MLA_QK) + EPS)
        qn = (qh * r) * gq
        qm_ref[hd, :, 0:MLA_NOPE] = qn[:, 0:MLA_NOPE].astype(BF16)
        qm_ref[hd, :, MLA_NOPE:MLA_QK_PAD] = rope(qn[:, MLA_NOPE:MLA_QK_PAD]).astype(BF16)

    ckv = _rms(proj(o4, o5), gkva_ref[...]).astype(BF16)
    kv = jnp.dot(ckv, wukv_ref[...], preferred_element_type=F32)
    kpe = proj(o5, o5 + LANES)
    ss_pe = jnp.sum(kpe * kpe, axis=-1, keepdims=True)
    gk = gk_ref[...]
    for hd in range(MLA_HEADS):
        base = hd * (MLA_NOPE + MLA_V)
        kn = kv[:, base:base + MLA_NOPE]
        r = lax.rsqrt((jnp.sum(kn * kn, axis=-1, keepdims=True) + ss_pe) * (1.0 / MLA_QK) + EPS)
        km_ref[hd, :, 0:MLA_NOPE] = ((kn * r) * gk[:, 0:MLA_NOPE]).astype(BF16)
        km_ref[hd, :, MLA_NOPE:MLA_QK_PAD] = rope((kpe * r) * gk[:, MLA_NOPE:MLA_QK_PAD]).astype(BF16)
        vm_ref[hd] = kv[:, base + MLA_NOPE:base + MLA_NOPE + MLA_V].astype(BF16)


def _proj(x, pos_blk, lw, cos_t, sin_t):
    n = x.shape[0]
    tm = TOKEN_TILE
    row = lambda i, pos: (i, 0)
    head = lambda i, pos: (0, i, 0)
    in_specs = [
        pl.BlockSpec((tm, D_MODEL), row),
        _const_spec((1, D_MODEL)),
        _const_spec(lw["w_in"].shape),
        _const_spec((NA_WIDTH, NA_WIDTH)),
        _const_spec((1, NA_WIDTH)),
        _const_spec((1, NA_WIDTH)),
        _const_spec((1, MLA_Q_LORA)),
        _const_spec(lw["w_uq"].shape),
        _const_spec((1, MLA_KV_LORA)),
        _const_spec(lw["w_ukv"].shape),
        _const_spec((1, MLA_QK_PAD)),
        _const_spec((1, MLA_QK_PAD)),
        pl.BlockSpec((tm, LANES), lambda i, pos: (pos[i], 0)),
        pl.BlockSpec((tm, LANES), lambda i, pos: (pos[i], 0)),
    ]
    out_specs = [
        pl.BlockSpec((tm, NA_WIDTH), row),
        pl.BlockSpec((tm, NA_WIDTH), row),
        pl.BlockSpec((tm, NA_WIDTH), row),
        pl.BlockSpec((MLA_HEADS, tm, MLA_QK_PAD), head),
        pl.BlockSpec((MLA_HEADS, tm, MLA_QK_PAD), head),
        pl.BlockSpec((MLA_HEADS, tm, MLA_V), head),
    ]
    out_shape = [
        jax.ShapeDtypeStruct((n, NA_WIDTH), BF16),
        jax.ShapeDtypeStruct((n, NA_WIDTH), BF16),
        jax.ShapeDtypeStruct((n, NA_WIDTH), BF16),
        jax.ShapeDtypeStruct((MLA_HEADS, n, MLA_QK_PAD), BF16),
        jax.ShapeDtypeStruct((MLA_HEADS, n, MLA_QK_PAD), BF16),
        jax.ShapeDtypeStruct((MLA_HEADS, n, MLA_V), BF16),
    ]
    return pl.pallas_call(
        _proj_kernel,
        grid_spec=pltpu.PrefetchScalarGridSpec(
            num_scalar_prefetch=1, grid=(n // tm,), in_specs=in_specs, out_specs=out_specs),
        out_shape=out_shape,
        compiler_params=_cparams(("arbitrary",)),
        name="proj",
    )(pos_blk, x, lw["g_mix"], lw["w_in"], lw["gsum"], lw["g_qna"], lw["g_kna"], lw["g_qa"], lw["w_uq"],
      lw["g_kva"], lw["w_ukv"], lw["g_q"], lw["g_k"], cos_t, sin_t)


def _na_kernel(w0_ref, kind_ref, q_ref, k_ref, v_ref, bias_ref, o_ref):
    del w0_ref
    kind = kind_ref[pl.program_id(0)]
    is_first = kind == 0
    is_last = kind == 2
    lane = lax.broadcasted_iota(jnp.int32, (GRID_W, LANES), 1)
    lo_half = lane < NA_HEAD_DIM
    half_rows = NA_ROWS // 2

    def one_row(j, carry):
        sh = j - half_rows
        off = jnp.where(is_first, jnp.maximum(sh, 0), jnp.where(is_last, NA_BLOCK_ROWS + jnp.minimum(sh, 0), j))
        dcls = jnp.where(is_first, jnp.minimum(j, half_rows), jnp.where(is_last, jnp.maximum(j, half_rows), half_rows))
        kstart = pl.multiple_of(off * GRID_W, GRID_W)
        qstart = pl.multiple_of(j * GRID_W, GRID_W)
        for p in range(NA_HEADS // 2):
            cols = slice(p * LANES, (p + 1) * LANES)
            qp = q_ref[pl.ds(qstart, GRID_W), cols]
            kp = k_ref[pl.ds(kstart, NA_ROWS * GRID_W), cols]
            vp = v_ref[pl.ds(kstart, NA_ROWS * GRID_W), cols]
            outs = []
            for half in range(2):
                keep = lo_half if half == 0 else jnp.logical_not(lo_half)
                qm = jnp.where(keep, qp, jnp.zeros_like(qp))
                s = lax.dot_general(qm, kp, (((1,), (1,)), ((), ())), preferred_element_type=F32)
                s = s + bias_ref[dcls, 2 * p + half]
                m = jnp.max(s, axis=-1, keepdims=True)
                e = jnp.exp(s - m)
                l = jnp.sum(e, axis=-1, keepdims=True)
                outs.append(jnp.dot(e.astype(BF16), vp, preferred_element_type=F32) / l)
            o_ref[pl.ds(qstart, GRID_W), cols] = jnp.where(lo_half, outs[0], outs[1])
        return carry

    lax.fori_loop(0, NA_BLOCK_ROWS, one_row, 0)


def _na_bias_table(rpb):
    c = jnp.arange(GRID_W)
    cs = jnp.clip(c - NA_COLS // 2, 0, GRID_W - NA_COLS)
    kc = jnp.arange(GRID_W)
    valid = (kc[None, :] >= cs[:, None]) & (kc[None, :] < cs[:, None] + NA_COLS)
    dc = jnp.clip(kc[None, :] - c[:, None] + (NA_COLS - 1), 0, 2 * NA_COLS - 2)
    d = jnp.arange(NA_ROWS)
    dr = jnp.arange(NA_ROWS)[None, :] - d[:, None] + (NA_ROWS - 1)
    t = rpb[:, dr[:, :, None, None], dc[None, None, :, :]]
    t = jnp.where(valid[None, None, None], t, NEG_INF)
    t = jnp.transpose(t, (1, 0, 3, 2, 4))
    return t.reshape(NA_ROWS, NA_HEADS, GRID_W, NA_ROWS * GRID_W).astype(F32)


def _na_block_tables(segments):
    w0, kind = [], []
    tok = 0
    for b, s in segments:
        rows = s // GRID_W
        nb = rows // NA_BLOCK_ROWS
        for _ in range(b):
            for i in range(nb):
                r0 = min(max(NA_BLOCK_ROWS * i - NA_ROWS // 2, 0), rows - NA_WINDOW_ROWS)
                w0.append(tok // GRID_W + r0)
                kind.append(0 if i == 0 else (2 if i == nb - 1 else 1))
            tok += s
    return jnp.asarray(w0, jnp.int32), jnp.asarray(kind, jnp.int32)


def _na(q, k, v, bias, segments):
    n = q.shape[0]
    tq = NA_BLOCK_ROWS * GRID_W
    tw = NA_WINDOW_ROWS * GRID_W
    w0, kind = _na_block_tables(segments)
    win = pl.BlockSpec((pl.Element(tw), pl.Element(NA_WIDTH)), lambda i, w0, kind: (w0[i] * GRID_W, 0))
    return pl.pallas_call(
        _na_kernel,
        grid_spec=pltpu.PrefetchScalarGridSpec(
            num_scalar_prefetch=2,
            grid=(n // tq,),
            in_specs=[
                pl.BlockSpec((tq, NA_WIDTH), lambda i, w0, kind: (i, 0)),
                win,
                win,
                pl.BlockSpec(bias.shape, lambda i, w0, kind: (0, 0, 0, 0), pipeline_mode=pl.Buffered(1)),
            ],
            out_specs=pl.BlockSpec((tq, NA_WIDTH), lambda i, w0, kind: (i, 0)),
        ),
        out_shape=jax.ShapeDtypeStruct((n, NA_WIDTH), F32),
        compiler_params=_cparams(("arbitrary",)),
        name="na",
    )(w0, kind, q, k, v, bias)


def _flash_kernel(prev_ref, q_ref, k_ref, v_ref, o_ref, m_ref, l_ref, acc_ref, *, n_chunks, tk):
    del prev_ref
    q = q_ref[...]
    m_ref[...] = jnp.full(m_ref.shape, -jnp.inf, F32)
    l_ref[...] = jnp.zeros(l_ref.shape, F32)
    acc_ref[...] = jnp.zeros(acc_ref.shape, F32)

    def body(c, carry):
        start = pl.multiple_of(c * tk, tk)
        k = k_ref[pl.ds(start, tk), :]
        v = v_ref[pl.ds(start, tk), :]
        s = lax.dot_general(q, k, (((1,), (1,)), ((), ())), preferred_element_type=F32)
        m_prev = m_ref[...]
        m_new = jnp.maximum(m_prev, jnp.max(s, axis=-1, keepdims=True))
        alpha = jnp.exp(m_prev - m_new)
        p = jnp.exp(s - jnp.concatenate([m_new] * (tk // LANES), axis=1))
        l_ref[...] = alpha * l_ref[...] + jnp.sum(p, axis=-1, keepdims=True)
        acc_ref[...] = alpha * acc_ref[...] + jnp.dot(p.astype(BF16), v, preferred_element_type=F32)
        m_ref[...] = m_new
        return carry

    lax.fori_loop(0, n_chunks, body, 0)
    o_ref[...] = acc_ref[...] / l_ref[...]


def _flash(prev, q, k, v, tok_off, b, s):
    n = q.shape[1]
    tq, tk = FLASH_TQ, FLASH_TK
    assert tok_off % s == 0 and s % tq == 0 and s % tk == 0
    nq = s // tq
    qblk0 = tok_off // tq
    sblk0 = tok_off // s
    return pl.pallas_call(
        functools.partial(_flash_kernel, n_chunks=s // tk, tk=tk),
        grid=(b, MLA_HEADS, nq),
        in_specs=[
            pl.BlockSpec(memory_space=pl.ANY),
            pl.BlockSpec((None, tq, MLA_QK_PAD), lambda bi, h, i: (h, qblk0 + bi * nq + i, 0)),
            pl.BlockSpec((None, s, MLA_QK_PAD), lambda bi, h, i: (h, sblk0 + bi, 0)),
            pl.BlockSpec((None, s, MLA_V), lambda bi, h, i: (h, sblk0 + bi, 0)),
        ],
        out_specs=pl.BlockSpec((tq, MLA_V), lambda bi, h, i: (qblk0 + bi * nq + i, h)),
        out_shape=jax.ShapeDtypeStruct((n, MLA_WIDTH), F32),
        scratch_shapes=[pltpu.VMEM((tq, LANES), F32), pltpu.VMEM((tq, LANES), F32), pltpu.VMEM((tq, MLA_V), F32)],
        input_output_aliases={0: 0},
        compiler_params=_cparams(("arbitrary", "arbitrary", "arbitrary")),
        name="flash",
    )(prev, q, k, v)


def _merge_kernel(*refs, route):
    if route:
        (x_ref, a_ref, b_ref, ga_ref, gb_ref, wout_ref, gffn_ref, wrh_ref, wrl_ref,
         xo_ref, h_ref, idx_ref, gate_ref) = refs
    else:
        x_ref, a_ref, b_ref, ga_ref, gb_ref, wout_ref, gffn_ref, xo_ref, h_ref = refs
    a = _rms(a_ref[...], ga_ref[...]).astype(BF16)
    b = _rms(b_ref[...], gb_ref[...]).astype(BF16)
    y = (jnp.dot(a, wout_ref[0:NA_WIDTH, :], preferred_element_type=F32)
         + jnp.dot(b, wout_ref[NA_WIDTH:NA_WIDTH + MLA_WIDTH, :], preferred_element_type=F32))
    xn = x_ref[...] + y
    xo_ref[...] = xn
    hf = _rms(xn, gffn_ref[...])
    h_ref[...] = hf.astype(h_ref.dtype)
    if route:
        hh, hl = _split_bf16(hf)
        wh = wrh_ref[...]
        logits = (jnp.dot(hh, wh, preferred_element_type=F32)
                  + (jnp.dot(hh, wrl_ref[...], preferred_element_type=F32)
                     + jnp.dot(hl, wh, preferred_element_type=F32)))
        lane = lax.broadcasted_iota(jnp.int32, logits.shape, 1)
        lg = jnp.where(lane < N_EXPERTS, logits, -jnp.inf)
        t1 = jnp.max(lg, axis=-1, keepdims=True)
        i1 = jnp.min(jnp.where(lg == t1, lane, LANES), axis=-1, keepdims=True)
        lg2 = jnp.where(lane == i1, -jnp.inf, lg)
        t2 = jnp.max(lg2, axis=-1, keepdims=True)
        i2 = jnp.min(jnp.where(lg2 == t2, lane, LANES), axis=-1, keepdims=True)
        e2 = jnp.exp(t2 - t1)
        den = 1.0 + e2
        idx_ref[...] = jnp.where(lane == 0, i1, jnp.where(lane == 1, i2, 0))
        gate_ref[...] = jnp.where(lane == 0, 1.0 / den, jnp.where(lane == 1, e2 / den, 0.0))


def _merge(x, out_a, out_b, lw, route):
    n = x.shape[0]
    tm = TOKEN_TILE
    row = lambda i: (i, 0)
    in_specs = [
        pl.BlockSpec((tm, D_MODEL), row),
        pl.BlockSpec((tm, NA_WIDTH), row),
        pl.BlockSpec((tm, MLA_WIDTH), row),
        _const_spec((1, NA_WIDTH)),
        _const_spec((1, MLA_WIDTH)),
        _const_spec((NA_WIDTH + MLA_WIDTH, D_MODEL)),
        _const_spec((1, D_MODEL)),
    ]
    args = [x, out_a, out_b, lw["g_oa"], lw["g_ob"], lw["w_out"], lw["g_ffn"]]
    out_specs = [pl.BlockSpec((tm, D_MODEL), row), pl.BlockSpec((tm, D_MODEL), row)]
    out_shape = [jax.ShapeDtypeStruct((n, D_MODEL), F32),
                 jax.ShapeDtypeStruct((n, D_MODEL), F32 if route else BF16)]
    if route:
        in_specs += [_const_spec((D_MODEL, LANES)), _const_spec((D_MODEL, LANES))]
        args += [lw["w_router_hi"], lw["w_router_lo"]]
        out_specs += [pl.BlockSpec((tm, LANES), row), pl.BlockSpec((tm, LANES), row)]
        out_shape += [jax.ShapeDtypeStruct((n, LANES), jnp.int32), jax.ShapeDtypeStruct((n, LANES), F32)]
    return pl.pallas_call(
        functools.partial(_merge_kernel, route=route),
        grid=(n // tm,),
        in_specs=in_specs,
        out_specs=out_specs,
        out_shape=out_shape,
        compiler_params=_cparams(("arbitrary",)),
        name="merge_route" if route else "merge",
    )(*args)


def _ffn_kernel(*refs, residual):
    if residual:
        te_ref, src_ref, nv_ref, h_ref, x_ref, wg_ref, wu_ref, wd_ref, o_ref = refs
    else:
        te_ref, src_ref, nv_ref, h_ref, wg_ref, wu_ref, wd_ref, o_ref = refs
    del te_ref, src_ref
    d_ff = wg_ref.shape[-1]

    @pl.when(pl.program_id(0) < nv_ref[0])
    def _():
        h = h_ref[...].astype(BF16)
        acc = x_ref[...] if residual else None
        for c in range(d_ff // FFN_CHUNK):
            sl = slice(c * FFN_CHUNK, (c + 1) * FFN_CHUNK)
            g = jnp.dot(h, wg_ref[:, sl], preferred_element_type=F32)
            u = jnp.dot(h, wu_ref[:, sl], preferred_element_type=F32)
            act = ((g * jax.nn.sigmoid(g)) * u).astype(BF16)
            y = jnp.dot(act, wd_ref[sl, :], preferred_element_type=F32)
            acc = y if acc is None else acc + y
        o_ref[...] = acc


def _ffn(h, x, wg, wu, wd, tile_expert, tile_src, n_valid):
    n = h.shape[0]
    tm = TOKEN_TILE
    d_ff = wg.shape[-1]
    assert d_ff % FFN_CHUNK == 0
    row = lambda i, te, src, nv: (src[i], 0)
    wspec = lambda shp: pl.BlockSpec((None,) + shp, lambda i, te, src, nv: (te[i], 0, 0),
                                     pipeline_mode=pl.Buffered(1))
    residual = x is not None
    in_specs = [pl.BlockSpec((tm, D_MODEL), row)]
    args = [h]
    if residual:
        in_specs.append(pl.BlockSpec((tm, D_MODEL), row))
        args.append(x)
    in_specs += [wspec((D_MODEL, d_ff)), wspec((D_MODEL, d_ff)), wspec((d_ff, D_MODEL))]
    args += [wg, wu, wd]
    return pl.pallas_call(
        functools.partial(_ffn_kernel, residual=residual),
        grid_spec=pltpu.PrefetchScalarGridSpec(
            num_scalar_prefetch=3, grid=(n // tm,), in_specs=in_specs,
            out_specs=pl.BlockSpec((tm, D_MODEL), row)),
        out_shape=jax.ShapeDtypeStruct((n, D_MODEL), F32),
        compiler_params=_cparams(("arbitrary",)),
        name="ffn" if residual else "expert_ffn",
    )(tile_expert, tile_src, n_valid, *args)


def _row_copy(src_ref, src_row, dst_ref, dst_row, sem):
    return pltpu.make_async_copy(src_ref.at[pl.ds(src_row, 1), :], dst_ref.at[pl.ds(dst_row, 1), :], sem)


def _dispatch_kernel(pos_ref, h_ref, zero_ref, xs_ref, sem):
    del zero_ref
    tm = h_ref.shape[0]

    def issue(r, c):
        for kk in range(TOP_K):
            _row_copy(h_ref, r, xs_ref, pos_ref[TOP_K * r + kk], sem).start()
        return c

    lax.fori_loop(0, tm, issue, 0)

    def drain(r, c):
        for kk in range(TOP_K):
            _row_copy(h_ref, r, xs_ref, pos_ref[TOP_K * r + kk], sem).wait()
        return c

    lax.fori_loop(0, tm, drain, 0)


def _dispatch(h, pos_flat, n_slots):
    n = h.shape[0]
    tm = TOKEN_TILE
    zeros = jnp.zeros((n_slots, D_MODEL), F32)
    return pl.pallas_call(
        _dispatch_kernel,
        grid=(n // tm,),
        in_specs=[
            pl.BlockSpec((TOP_K * tm,), lambda i: (i,), memory_space=pltpu.SMEM),
            pl.BlockSpec((tm, D_MODEL), lambda i: (i, 0)),
            pl.BlockSpec(memory_space=pl.ANY),
        ],
        out_specs=pl.BlockSpec(memory_space=pl.ANY),
        out_shape=jax.ShapeDtypeStruct((n_slots, D_MODEL), F32),
        scratch_shapes=[pltpu.SemaphoreType.DMA(())],
        input_output_aliases={2: 0},
        compiler_params=_cparams(("arbitrary",)),
        name="dispatch",
    )(pos_flat, h, zeros)


def _combine_kernel(pos_ref, x_ref, gate_ref, ys_ref, o_ref, buf_ref, sem):
    tm = x_ref.shape[0]

    def issue(r, c):
        for kk in range(TOP_K):
            _row_copy(ys_ref, pos_ref[TOP_K * r + kk], buf_ref.at[kk], r, sem).start()
        return c

    lax.fori_loop(0, tm, issue, 0)

    def drain(r, c):
        for kk in range(TOP_K):
            _row_copy(ys_ref, pos_ref[TOP_K * r + kk], buf_ref.at[kk], r, sem).wait()
        return c

    lax.fori_loop(0, tm, drain, 0)
    gate = gate_ref[...]
    out = gate[:, 0:1] * buf_ref[0] + gate[:, 1:2] * buf_ref[1]
    o_ref[...] = x_ref[...] + out


def _combine(x, gate_sorted, ys, pos_flat):
    n = x.shape[0]
    tm = TOKEN_TILE
    return pl.pallas_call(
        _combine_kernel,
        grid=(n // tm,),
        in_specs=[
            pl.BlockSpec((TOP_K * tm,), lambda i: (i,), memory_space=pltpu.SMEM),
            pl.BlockSpec((tm, D_MODEL), lambda i: (i, 0)),
            pl.BlockSpec((tm, LANES), lambda i: (i, 0)),
            pl.BlockSpec(memory_space=pl.ANY),
        ],
        out_specs=pl.BlockSpec((tm, D_MODEL), lambda i: (i, 0)),
        out_shape=jax.ShapeDtypeStruct((n, D_MODEL), F32),
        scratch_shapes=[pltpu.VMEM((TOP_K, tm, D_MODEL), F32), pltpu.SemaphoreType.DMA(())],
        compiler_params=_cparams(("arbitrary",)),
        name="combine",
    )(pos_flat, x, gate_sorted, ys)


def _moe(x, h, idx, gate, wg, wu, wd):
    n = x.shape[0]
    tm = TOKEN_TILE
    n_tiles = TOP_K * n // tm + N_EXPERTS
    n_slots = n_tiles * tm
    first = idx[:, 0] < idx[:, 1]
    e_lo = jnp.where(first, idx[:, 0], idx[:, 1])
    e_hi = jnp.where(first, idx[:, 1], idx[:, 0])
    g_lo = jnp.where(first, gate[:, 0], gate[:, 1])
    g_hi = jnp.where(first, gate[:, 1], gate[:, 0])
    experts = jnp.arange(N_EXPERTS, dtype=jnp.int32)
    sel = ((e_lo[:, None] == experts[None, :]) | (e_hi[:, None] == experts[None, :])).astype(jnp.int32)
    csum = jnp.cumsum(sel, axis=0)
    rank = csum - sel
    counts = csum[-1]
    padded = ((counts + tm - 1) // tm) * tm
    ends = jnp.cumsum(padded)
    starts = ends - padded
    slot0 = starts[None, :] + rank
    pos_lo = jnp.take_along_axis(slot0, e_lo[:, None], axis=1)[:, 0]
    pos_hi = jnp.take_along_axis(slot0, e_hi[:, None], axis=1)[:, 0]
    pos_flat = jnp.stack([pos_lo, pos_hi], axis=1).reshape(-1).astype(jnp.int32)
    tile_start = jnp.arange(n_tiles, dtype=jnp.int32) * tm
    n_valid = (ends[-1] // tm).astype(jnp.int32)
    tile_src = jnp.minimum(jnp.arange(n_tiles, dtype=jnp.int32), n_valid - 1)
    tile_expert = jnp.minimum(
        jnp.sum((tile_src[:, None] * tm >= ends[None, :]).astype(jnp.int32), axis=1), N_EXPERTS - 1).astype(jnp.int32)
    del tile_start
    gate_sorted = jnp.zeros((n, LANES), F32).at[:, 0].set(g_lo).at[:, 1].set(g_hi)
    xs = _dispatch(h, pos_flat, n_slots)
    ys = _ffn(xs, None, wg, wu, wd, tile_expert, tile_src, n_valid.reshape(1))
    return _combine(x, gate_sorted, ys, pos_flat)


def _rope_layout(w64):
    z = jnp.zeros(w64.shape[:-1] + (MLA_ROPE // 2,), w64.dtype)
    return jnp.concatenate([w64[..., :MLA_ROPE // 2], z, w64[..., MLA_ROPE // 2:], z], axis=-1)


def _qk_layout(w192):
    return jnp.concatenate([w192[..., :MLA_NOPE], _rope_layout(w192[..., MLA_NOPE:])], axis=-1)


def _layer_weights(l, norm_mix, norm_ffn, w_in, na_q_norm, na_k_norm, mla_q_a_norm, w_uq, mla_kv_a_norm, w_ukv,
                   mla_q_norm, mla_k_norm, na_out_norm, mla_out_norm, w_out):
    o5 = 3 * NA_WIDTH + MLA_Q_LORA + MLA_KV_LORA
    w_in_l = w_in[l]
    w_in_p = jnp.concatenate([w_in_l[:, :o5], _rope_layout(w_in_l[:, o5:])], axis=1)
    w_uq_p = _qk_layout(w_uq[l].reshape(MLA_Q_LORA, MLA_HEADS, MLA_QK)).reshape(MLA_Q_LORA, MLA_HEADS * MLA_QK_PAD)
    grp = jnp.arange(NA_WIDTH) // NA_HEAD_DIM
    row = lambda v: v.reshape(1, -1).astype(F32)
    return {
        "g_mix": row(norm_mix[l]),
        "g_ffn": row(norm_ffn[l]),
        "w_in": w_in_p.astype(BF16),
        "gsum": (grp[:, None] == grp[None, :]).astype(BF16),
        "g_qna": row(jnp.tile(na_q_norm[l], NA_HEADS) * (NA_HEAD_DIM ** -0.5)),
        "g_kna": row(jnp.tile(na_k_norm[l], NA_HEADS)),
        "g_qa": row(mla_q_a_norm[l]),
        "w_uq": w_uq_p.astype(BF16),
        "g_kva": row(mla_kv_a_norm[l]),
        "w_ukv": w_ukv[l].astype(BF16),
        "g_q": row(_qk_layout(mla_q_norm[l]) * (MLA_QK ** -0.5)),
        "g_k": row(_qk_layout(mla_k_norm[l])),
        "g_oa": row(na_out_norm[l]),
        "g_ob": row(mla_out_norm[l]),
        "w_out": w_out[l].astype(BF16),
    }


def _rope_tables(s_max):
    inv = ROPE_THETA ** (-jnp.arange(0, MLA_ROPE, 2, dtype=F32) / MLA_ROPE)
    ang = jnp.arange(s_max, dtype=F32)[:, None] * inv[None, :]
    cos, sin = jnp.cos(ang), jnp.sin(ang)
    z = jnp.zeros_like(cos)
    return jnp.concatenate([cos, z, cos, z], axis=1), jnp.concatenate([-sin, z, sin, z], axis=1)


def _trunk(x, segments, norm_mix, norm_ffn, w_in, na_q_norm, na_k_norm, na_rpb, mla_q_a_norm, w_uq, mla_kv_a_norm,
           w_ukv, mla_q_norm, mla_k_norm, na_out_norm, mla_out_norm, w_out, ffn_w_gate, ffn_w_up, ffn_w_down,
           moe_router, moe_w_gate, moe_w_up, moe_w_down):
    n = x.shape[0]
    tm = TOKEN_TILE
    depth = w_in.shape[0]
    s_max = max(s for _, s in segments)
    cos_t, sin_t = _rope_tables(s_max)
    pos_blk = jnp.asarray([i for b, s in segments for _ in range(b) for i in range(s // tm)], jnp.int32)
    n_tiles = n // tm
    dense_tiles = (jnp.zeros((n_tiles,), jnp.int32), jnp.arange(n_tiles, dtype=jnp.int32),
                   jnp.full((1,), n_tiles, jnp.int32))
    for l in range(depth):
        lw = _layer_weights(l, norm_mix, norm_ffn, w_in, na_q_norm, na_k_norm, mla_q_a_norm, w_uq, mla_kv_a_norm,
                            w_ukv, mla_q_norm, mla_k_norm, na_out_norm, mla_out_norm, w_out)
        qna, kna, vna, qm, km, vm = _proj(x, pos_blk, lw, cos_t, sin_t)
        out_a = _na(qna, kna, vna, _na_bias_table(na_rpb[l]), segments)
        out_b = jnp.zeros((n, MLA_WIDTH), F32)
        tok = 0
        for b, s in segments:
            out_b = _flash(out_b, qm, km, vm, tok, b, s)
            tok += b * s
        i = l // 2
        if l % 2 == 0:
            x, h = _merge(x, out_a, out_b, lw, route=False)
            x = _ffn(h, x, ffn_w_gate[i:i + 1].astype(BF16), ffn_w_up[i:i + 1].astype(BF16),
                     ffn_w_down[i:i + 1].astype(BF16), *dense_tiles)
        else:
            wr = jnp.pad(moe_router[i], ((0, 0), (0, LANES - N_EXPERTS)))
            lw["w_router_hi"], lw["w_router_lo"] = _split_bf16(wr)
            x, h, idx, gate = _merge(x, out_a, out_b, lw, route=True)
            x = _moe(x, h, idx[:, :TOP_K], gate[:, :TOP_K], moe_w_gate[i].astype(BF16), moe_w_up[i].astype(BF16),
                     moe_w_down[i].astype(BF16))
    return x


def kernel(x_prompt, x_sample, norm_mix, norm_ffn, w_in, na_q_norm, na_k_norm, na_rpb, mla_q_a_norm, w_uq,
           mla_kv_a_norm, w_ukv, mla_q_norm, mla_k_norm, na_out_norm, mla_out_norm, w_out, ffn_w_gate, ffn_w_up,
           ffn_w_down, moe_router, moe_w_gate, moe_w_up, moe_w_down):
    bp, sp, d = x_prompt.shape
    bs, ss, _ = x_sample.shape
    segments = ((bp, sp), (bs, ss))
    x = jnp.concatenate([x_prompt.reshape(bp * sp, d), x_sample.reshape(bs * ss, d)], axis=0)
    y = _trunk(x, segments, norm_mix, norm_ffn, w_in, na_q_norm, na_k_norm, na_rpb, mla_q_a_norm, w_uq,
               mla_kv_a_norm, w_ukv, mla_q_norm, mla_k_norm, na_out_norm, mla_out_norm, w_out, ffn_w_gate,
               ffn_w_up, ffn_w_down, moe_router, moe_w_gate, moe_w_up, moe_w_down)
    n_p = bp * sp
    return y[:n_p].reshape(bp, sp, d), y[n_p:].reshape(bs, ss, d)
```

```python
import functools
import math

import jax
import jax.numpy as jnp
import numpy as np
from jax import lax
from jax.experimental import pallas as pl
from jax.experimental.pallas import tpu as pltpu

F32 = jnp.float32
BF16 = jnp.bfloat16

D_MODEL = 1024
GRID_W = 64
NA_HEADS = 8
NA_HEAD_DIM = 64
NA_ROWS = 8
NA_COLS = 16
NA_WIDTH = NA_HEADS * NA_HEAD_DIM
MLA_HEADS = 4
MLA_Q_LORA = 384
MLA_KV_LORA = 256
MLA_NOPE = 128
MLA_ROPE = 64
MLA_V = 128
MLA_QK = MLA_NOPE + MLA_ROPE
MLA_WIDTH = MLA_HEADS * MLA_V
MLA_QK_PAD = 256
ROPE_THETA = 10000.0
N_EXPERTS = 8
TOP_K = 2
EPS = 1e-6
NEG_INF = -1e30
LOG2E = math.log2(math.e)

LANES = 128
TOKEN_TILE = 512
NA_BLOCK_ROWS = 8
NA_WINDOW_ROWS = 16
FLASH_TQ = 1024
FLASH_TK = TOKEN_TILE
FFN_CHUNK = 1408
VMEM_LIMIT = 56 * 1024 * 1024


def _cparams(sem):
    return pltpu.CompilerParams(dimension_semantics=sem, vmem_limit_bytes=VMEM_LIMIT)


def _const_spec(shape):
    nd = len(shape)
    return pl.BlockSpec(shape, lambda *_: (0,) * nd, pipeline_mode=pl.Buffered(1))


def _rms(xf, g):
    y = xf * lax.rsqrt(jnp.mean(xf * xf, axis=-1, keepdims=True) + EPS)
    return y * g


def _split_bf16(v):
    hi = v.astype(BF16)
    lo = (v - hi.astype(F32)).astype(BF16)
    return hi, lo


def _proj_kernel(pos_ref, x_ref, gmix_ref, win_ref, gsum_ref, gqna_ref, gkna_ref, gqa_ref, wuq_ref,
                 gkva_ref, wukv_ref, gq_ref, gk_ref, cos_ref, sin_ref,
                 qna_ref, kna_ref, vna_ref, qm_ref, km_ref, vm_ref):
    del pos_ref
    h = _rms(x_ref[...], gmix_ref[...]).astype(BF16)

    def proj(lo, hi):
        return jnp.dot(h, win_ref[:, lo:hi], preferred_element_type=F32)

    gsum = gsum_ref[...]

    def head_norm(v, gain):
        hi, lo = _split_bf16(v * v)
        ss = jnp.dot(hi, gsum, preferred_element_type=F32) + jnp.dot(lo, gsum, preferred_element_type=F32)
        return (v * lax.rsqrt(ss * (1.0 / NA_HEAD_DIM) + EPS)) * gain

    o1, o2, o3 = NA_WIDTH, 2 * NA_WIDTH, 3 * NA_WIDTH
    o4 = o3 + MLA_Q_LORA
    o5 = o4 + MLA_KV_LORA
    qna_ref[...] = head_norm(proj(0, o1), gqna_ref[...]).astype(BF16)
    kna_ref[...] = head_norm(proj(o1, o2), gkna_ref[...]).astype(BF16)
    vna_ref[...] = proj(o2, o3).astype(BF16)

    cos = cos_ref[...]
    sin = sin_ref[...]

    def rope(u):
        return u * cos + pltpu.roll(u, 64, 1) * sin

    cq = _rms(proj(o3, o4), gqa_ref[...]).astype(BF16)
    q_all = jnp.dot(cq, wuq_ref[...], preferred_element_type=F32)
    gq = gq_ref[...]
    for hd in range(MLA_HEADS):
        qh = q_all[:, hd * MLA_QK_PAD:(hd + 1) * MLA_QK_PAD]
        r = lax.rsqrt(jnp.sum(qh * qh, axis=-1, keepdims=True) * (1.0 / MLA_QK) + EPS)
        qn = (qh * r) * gq
        qm_ref[hd, 0:MLA_NOPE, :] = qn[:, 0:MLA_NOPE].T.astype(BF16)
        qm_ref[hd, MLA_NOPE:MLA_QK_PAD, :] = rope(qn[:, MLA_NOPE:MLA_QK_PAD]).T.astype(BF16)

    ckv = _rms(proj(o4, o5), gkva_ref[...]).astype(BF16)
    kv = jnp.dot(ckv, wukv_ref[...], preferred_element_type=F32)
    kpe = proj(o5, o5 + LANES)
    ss_pe = jnp.sum(kpe * kpe, axis=-1, keepdims=True)
    gk = gk_ref[...]
    for hd in range(MLA_HEADS):
        base = hd * (MLA_NOPE + MLA_V)
        kn = kv[:, base:base + MLA_NOPE]
        r = lax.rsqrt((jnp.sum(kn * kn, axis=-1, keepdims=True) + ss_pe) * (1.0 / MLA_QK) + EPS)
        km_ref[hd, :, 0:MLA_NOPE] = ((kn * r) * gk[:, 0:MLA_NOPE]).astype(BF16)
        km_ref[hd, :, MLA_NOPE:MLA_QK_PAD] = rope((kpe * r) * gk[:, MLA_NOPE:MLA_QK_PAD]).astype(BF16)
        vm_ref[hd, 0] = kv[:, base + MLA_NOPE:base + MLA_NOPE + MLA_V].T.astype(BF16)


def _proj(x, pos_blk, lw, cos_t, sin_t):
    n = x.shape[0]
    tm = TOKEN_TILE
    row = lambda i, pos: (i, 0)
    head = lambda i, pos: (0, i, 0)
    in_specs = [
        pl.BlockSpec((tm, D_MODEL), row),
        _const_spec((1, D_MODEL)),
        _const_spec(lw["w_in"].shape),
        _const_spec((NA_WIDTH, NA_WIDTH)),
        _const_spec((1, NA_WIDTH)),
        _const_spec((1, NA_WIDTH)),
        _const_spec((1, MLA_Q_LORA)),
        _const_spec(lw["w_uq"].shape),
        _const_spec((1, MLA_KV_LORA)),
        _const_spec(lw["w_ukv"].shape),
        _const_spec((1, MLA_QK_PAD)),
        _const_spec((1, MLA_QK_PAD)),
        pl.BlockSpec((tm, LANES), lambda i, pos: (pos[i], 0)),
        pl.BlockSpec((tm, LANES), lambda i, pos: (pos[i], 0)),
    ]
    out_specs = [
        pl.BlockSpec((tm, NA_WIDTH), row),
        pl.BlockSpec((tm, NA_WIDTH), row),
        pl.BlockSpec((tm, NA_WIDTH), row),
        pl.BlockSpec((MLA_HEADS, MLA_QK_PAD, tm), lambda i, pos: (0, 0, i)),
        pl.BlockSpec((MLA_HEADS, tm, MLA_QK_PAD), head),
        pl.BlockSpec((MLA_HEADS, 1, MLA_V, tm), lambda i, pos: (0, i, 0, 0)),
    ]
    out_shape = [
        jax.ShapeDtypeStruct((n, NA_WIDTH), BF16),
        jax.ShapeDtypeStruct((n, NA_WIDTH), BF16),
        jax.ShapeDtypeStruct((n, NA_WIDTH), BF16),
        jax.ShapeDtypeStruct((MLA_HEADS, MLA_QK_PAD, n), BF16),
        jax.ShapeDtypeStruct((MLA_HEADS, n, MLA_QK_PAD), BF16),
        jax.ShapeDtypeStruct((MLA_HEADS, n // tm, MLA_V, tm), BF16),
    ]
    return pl.pallas_call(
        _proj_kernel,
        grid_spec=pltpu.PrefetchScalarGridSpec(
            num_scalar_prefetch=1, grid=(n // tm,), in_specs=in_specs, out_specs=out_specs),
        out_shape=out_shape,
        compiler_params=_cparams(("arbitrary",)),
        name="proj",
    )(pos_blk, x, lw["g_mix"], lw["w_in"], lw["gsum"], lw["g_qna"], lw["g_kna"], lw["g_qa"], lw["w_uq"],
      lw["g_kva"], lw["w_ukv"], lw["g_q"], lw["g_k"], cos_t, sin_t)


def _na_kernel(w0_ref, kind_ref, q_ref, k_ref, v_ref, bias_ref, o_ref):
    del w0_ref
    kind = kind_ref[pl.program_id(0)]
    is_first = kind == 0
    is_last = kind == 2
    lane = lax.broadcasted_iota(jnp.int32, (GRID_W, LANES), 1)
    lo_half = lane < NA_HEAD_DIM
    half_rows = NA_ROWS // 2

    def one_row(j, carry):
        sh = j - half_rows
        off = jnp.where(is_first, jnp.maximum(sh, 0), jnp.where(is_last, NA_BLOCK_ROWS + jnp.minimum(sh, 0), j))
        dcls = jnp.where(is_first, jnp.minimum(j, half_rows), jnp.where(is_last, jnp.maximum(j, half_rows), half_rows))
        kstart = pl.multiple_of(off * GRID_W, GRID_W)
        qstart = pl.multiple_of(j * GRID_W, GRID_W)
        scores = []
        for hd in range(NA_HEADS):
            cols = slice((hd // 2) * LANES, (hd // 2 + 1) * LANES)
            qp = q_ref[pl.ds(qstart, GRID_W), cols]
            kp = k_ref[pl.ds(kstart, NA_ROWS * GRID_W), cols]
            keep = lo_half if hd % 2 == 0 else jnp.logical_not(lo_half)
            qm = jnp.where(keep, qp, jnp.zeros_like(qp))
            s = lax.dot_general(qm, kp, (((1,), (1,)), ((), ())), preferred_element_type=F32)
            scores.append(s + bias_ref[dcls, hd])
        s = jnp.concatenate(scores, axis=0)
        e = jnp.exp2(s - jnp.max(s, axis=-1, keepdims=True))
        inv_l = 1.0 / jnp.sum(e, axis=-1, keepdims=True)
        pb = e.astype(BF16)
        for p in range(NA_HEADS // 2):
            cols = slice(p * LANES, (p + 1) * LANES)
            vp = v_ref[pl.ds(kstart, NA_ROWS * GRID_W), cols]
            outs = []
            for hd in (2 * p, 2 * p + 1):
                rows = slice(hd * GRID_W, (hd + 1) * GRID_W)
                outs.append(jnp.dot(pb[rows], vp, preferred_element_type=F32) * inv_l[rows])
            o_ref[pl.ds(qstart, GRID_W), cols] = jnp.where(lo_half, outs[0], outs[1])
        return carry

    lax.fori_loop(0, NA_BLOCK_ROWS, one_row, 0)


def _na_bias_table(rpb):
    c = np.arange(GRID_W)
    cs = np.clip(c - NA_COLS // 2, 0, GRID_W - NA_COLS)
    kc = np.arange(GRID_W)
    valid = (kc[None, :] >= cs[:, None]) & (kc[None, :] < cs[:, None] + NA_COLS)
    dc = kc[None, :] - c[:, None] + (NA_COLS - 1)
    onehot = ((np.arange(2 * NA_COLS - 1)[:, None, None] == dc[None]) & valid[None]).astype(np.float32)
    t = jnp.einsum("hrd,dck->hrck", rpb.astype(F32), jnp.asarray(onehot), precision=lax.Precision.HIGHEST)
    t = jnp.where(jnp.asarray(valid)[None, None], t * LOG2E, NEG_INF)
    tab = jnp.stack([t[:, NA_ROWS - 1 - d:2 * NA_ROWS - 1 - d] for d in range(NA_ROWS)], axis=0)
    tab = jnp.transpose(tab, (0, 1, 3, 2, 4))
    return tab.reshape(NA_ROWS, NA_HEADS, GRID_W, NA_ROWS * GRID_W)


def _na_block_tables(segments):
    w0, kind = [], []
    tok = 0
    for b, s in segments:
        rows = s // GRID_W
        nb = rows // NA_BLOCK_ROWS
        for _ in range(b):
            for i in range(nb):
                r0 = min(max(NA_BLOCK_ROWS * i - NA_ROWS // 2, 0), rows - NA_WINDOW_ROWS)
                w0.append(tok // GRID_W + r0)
                kind.append(0 if i == 0 else (2 if i == nb - 1 else 1))
            tok += s
    return jnp.asarray(w0, jnp.int32), jnp.asarray(kind, jnp.int32)


def _na(q, k, v, bias, segments):
    n = q.shape[0]
    tq = NA_BLOCK_ROWS * GRID_W
    tw = NA_WINDOW_ROWS * GRID_W
    w0, kind = _na_block_tables(segments)
    win = pl.BlockSpec((pl.Element(tw), pl.Element(NA_WIDTH)), lambda i, w0, kind: (w0[i] * GRID_W, 0))
    return pl.pallas_call(
        _na_kernel,
        grid_spec=pltpu.PrefetchScalarGridSpec(
            num_scalar_prefetch=2,
            grid=(n // tq,),
            in_specs=[
                pl.BlockSpec((tq, NA_WIDTH), lambda i, w0, kind: (i, 0)),
                win,
                win,
                pl.BlockSpec(bias.shape, lambda i, w0, kind: (0, 0, 0, 0), pipeline_mode=pl.Buffered(1)),
            ],
            out_specs=pl.BlockSpec((tq, NA_WIDTH), lambda i, w0, kind: (i, 0)),
        ),
        out_shape=jax.ShapeDtypeStruct((n, NA_WIDTH), F32),
        compiler_params=_cparams(("arbitrary",)),
        name="na",
    )(w0, kind, q, k, v, bias)


def _flash_kernel(prev_ref, qt_ref, k_ref, vt_ref, o_ref, s_scr, acc_ref, *, n_chunks, tk):
    del prev_ref
    acc_ref[...] = jnp.zeros(acc_ref.shape, F32)
    qt = qt_ref[...]
    tq = qt.shape[1]

    def scores(c, slot):
        start = pl.multiple_of(c * tk, tk)
        s_scr[slot] = jnp.dot(k_ref[pl.ds(start, tk), :], qt, preferred_element_type=F32)

    def softmax_pv(c, slot, m_prev, l_prev):
        st = s_scr[slot]
        m_new = jnp.maximum(m_prev, jnp.max(st, axis=0, keepdims=True))
        alpha = jnp.exp2(m_prev - m_new)
        pt = jnp.exp2(st - m_new)
        l_new = alpha * l_prev + jnp.sum(pt, axis=0, keepdims=True)
        acc_ref[...] = alpha * acc_ref[...] + jnp.dot(vt_ref[c], pt.astype(BF16), preferred_element_type=F32)
        return m_new, l_new

    scores(0, 0)

    def body(i, carry):
        m, l = carry
        c0 = 2 * i
        scores(c0 + 1, 1)
        m, l = softmax_pv(c0, 0, m, l)
        scores(jnp.minimum(c0 + 2, n_chunks - 1), 0)
        m, l = softmax_pv(c0 + 1, 1, m, l)
        return m, l

    init = (jnp.full((1, tq), -jnp.inf, F32), jnp.zeros((1, tq), F32))
    _, l = lax.fori_loop(0, n_chunks // 2, body, init)
    o_ref[...] = (acc_ref[...] / l).T


def _flash(prev, qt, k, vt, tok_off, b, s):
    n = k.shape[1]
    tq, tk = FLASH_TQ, FLASH_TK
    assert tok_off % s == 0 and s % tq == 0 and s % (2 * tk) == 0 and vt.shape[-1] == tk
    nq = s // tq
    nc = s // tk
    qblk0 = tok_off // tq
    sblk0 = tok_off // s
    return pl.pallas_call(
        functools.partial(_flash_kernel, n_chunks=nc, tk=tk),
        grid=(b, MLA_HEADS, nq),
        in_specs=[
            pl.BlockSpec(memory_space=pl.ANY),
            pl.BlockSpec((None, MLA_QK_PAD, tq), lambda bi, h, i: (h, 0, qblk0 + bi * nq + i)),
            pl.BlockSpec((None, s, MLA_QK_PAD), lambda bi, h, i: (h, sblk0 + bi, 0)),
            pl.BlockSpec((None, nc, MLA_V, tk), lambda bi, h, i: (h, sblk0 + bi, 0, 0)),
        ],
        out_specs=pl.BlockSpec((tq, MLA_V), lambda bi, h, i: (qblk0 + bi * nq + i, h)),
        out_shape=jax.ShapeDtypeStruct((n, MLA_WIDTH), F32),
        scratch_shapes=[pltpu.VMEM((2, tk, tq), F32), pltpu.VMEM((MLA_V, tq), F32)],
        input_output_aliases={0: 0},
        compiler_params=_cparams(("arbitrary", "arbitrary", "arbitrary")),
        name="flash",
    )(prev, qt, k, vt)


def _merge_kernel(*refs, route):
    if route:
        (x_ref, a_ref, b_ref, ga_ref, gb_ref, wout_ref, gffn_ref, wrh_ref, wrl_ref,
         xo_ref, h_ref, idx_ref, gate_ref) = refs
    else:
        x_ref, a_ref, b_ref, ga_ref, gb_ref, wout_ref, gffn_ref, xo_ref, h_ref = refs
    a = _rms(a_ref[...], ga_ref[...]).astype(BF16)
    b = _rms(b_ref[...], gb_ref[...]).astype(BF16)
    y = (jnp.dot(a, wout_ref[0:NA_WIDTH, :], preferred_element_type=F32)
         + jnp.dot(b, wout_ref[NA_WIDTH:NA_WIDTH + MLA_WIDTH, :], preferred_element_type=F32))
    xn = x_ref[...] + y
    xo_ref[...] = xn
    hf = _rms(xn, gffn_ref[...])
    h_ref[...] = hf.astype(h_ref.dtype)
    if route:
        hh, hl = _split_bf16(hf)
        wh = wrh_ref[...]
        logits = (jnp.dot(hh, wh, preferred_element_type=F32)
                  + (jnp.dot(hh, wrl_ref[...], preferred_element_type=F32)
                     + jnp.dot(hl, wh, preferred_element_type=F32)))
        lane = lax.broadcasted_iota(jnp.int32, logits.shape, 1)
        lg = jnp.where(lane < N_EXPERTS, logits, -jnp.inf)
        t1 = jnp.max(lg, axis=-1, keepdims=True)
        i1 = jnp.min(jnp.where(lg == t1, lane, LANES), axis=-1, keepdims=True)
        lg2 = jnp.where(lane == i1, -jnp.inf, lg)
        t2 = jnp.max(lg2, axis=-1, keepdims=True)
        i2 = jnp.min(jnp.where(lg2 == t2, lane, LANES), axis=-1, keepdims=True)
        e2 = jnp.exp(t2 - t1)
        den = 1.0 + e2
        idx_ref[...] = jnp.where(lane == 0, i1, jnp.where(lane == 1, i2, 0))
        gate_ref[...] = jnp.where(lane == 0, 1.0 / den, jnp.where(lane == 1, e2 / den, 0.0))


def _merge(x, out_a, out_b, lw, route):
    n = x.shape[0]
    tm = TOKEN_TILE
    row = lambda i: (i, 0)
    in_specs = [
        pl.BlockSpec((tm, D_MODEL), row),
        pl.BlockSpec((tm, NA_WIDTH), row),
        pl.BlockSpec((tm, MLA_WIDTH), row),
        _const_spec((1, NA_WIDTH)),
        _const_spec((1, MLA_WIDTH)),
        _const_spec((NA_WIDTH + MLA_WIDTH, D_MODEL)),
        _const_spec((1, D_MODEL)),
    ]
    args = [x, out_a, out_b, lw["g_oa"], lw["g_ob"], lw["w_out"], lw["g_ffn"]]
    out_specs = [pl.BlockSpec((tm, D_MODEL), row), pl.BlockSpec((tm, D_MODEL), row)]
    out_shape = [jax.ShapeDtypeStruct((n, D_MODEL), F32),
                 jax.ShapeDtypeStruct((n, D_MODEL), F32 if route else BF16)]
    if route:
        in_specs += [_const_spec((D_MODEL, LANES)), _const_spec((D_MODEL, LANES))]
        args += [lw["w_router_hi"], lw["w_router_lo"]]
        out_specs += [pl.BlockSpec((tm, LANES), row), pl.BlockSpec((tm, LANES), row)]
        out_shape += [jax.ShapeDtypeStruct((n, LANES), jnp.int32), jax.ShapeDtypeStruct((n, LANES), F32)]
    return pl.pallas_call(
        functools.partial(_merge_kernel, route=route),
        grid=(n // tm,),
        in_specs=in_specs,
        out_specs=out_specs,
        out_shape=out_shape,
        compiler_params=_cparams(("arbitrary",)),
        name="merge_route" if route else "merge",
    )(*args)


def _ffn_kernel(*refs, residual):
    if residual:
        te_ref, src_ref, nv_ref, h_ref, x_ref, wg_ref, wu_ref, wd_ref, o_ref = refs
    else:
        te_ref, src_ref, nv_ref, h_ref, wg_ref, wu_ref, wd_ref, o_ref = refs
    del te_ref, src_ref
    d_ff = wg_ref.shape[-1]

    @pl.when(pl.program_id(0) < nv_ref[0])
    def _():
        h = h_ref[...].astype(BF16)
        acc = x_ref[...] if residual else None
        for c in range(d_ff // FFN_CHUNK):
            sl = slice(c * FFN_CHUNK, (c + 1) * FFN_CHUNK)
            g = jnp.dot(h, wg_ref[:, sl], preferred_element_type=F32)
            u = jnp.dot(h, wu_ref[:, sl], preferred_element_type=F32)
            act = ((g * jax.nn.sigmoid(g)) * u).astype(BF16)
            y = jnp.dot(act, wd_ref[sl, :], preferred_element_type=F32)
            acc = y if acc is None else acc + y
        o_ref[...] = acc


def _ffn(h, x, wg, wu, wd, tile_expert, tile_src, n_valid):
    n = h.shape[0]
    tm = TOKEN_TILE
    d_ff = wg.shape[-1]
    assert d_ff % FFN_CHUNK == 0
    row = lambda i, te, src, nv: (src[i], 0)
    wspec = lambda shp: pl.BlockSpec((None,) + shp, lambda i, te, src, nv: (te[i], 0, 0),
                                     pipeline_mode=pl.Buffered(1))
    residual = x is not None
    in_specs = [pl.BlockSpec((tm, D_MODEL), row)]
    args = [h]
    if residual:
        in_specs.append(pl.BlockSpec((tm, D_MODEL), row))
        args.append(x)
    in_specs += [wspec((D_MODEL, d_ff)), wspec((D_MODEL, d_ff)), wspec((d_ff, D_MODEL))]
    args += [wg, wu, wd]
    return pl.pallas_call(
        functools.partial(_ffn_kernel, residual=residual),
        grid_spec=pltpu.PrefetchScalarGridSpec(
            num_scalar_prefetch=3, grid=(n // tm,), in_specs=in_specs,
            out_specs=pl.BlockSpec((tm, D_MODEL), row)),
        out_shape=jax.ShapeDtypeStruct((n, D_MODEL), F32),
        compiler_params=_cparams(("arbitrary",)),
        name="ffn" if residual else "expert_ffn",
    )(tile_expert, tile_src, n_valid, *args)


def _row_copy(src_ref, src_row, dst_ref, dst_row, sem):
    return pltpu.make_async_copy(src_ref.at[pl.ds(src_row, 1), :], dst_ref.at[pl.ds(dst_row, 1), :], sem)


def _dispatch_kernel(pos_ref, h_ref, zero_ref, xs_ref, sem):
    del zero_ref
    tm = h_ref.shape[0]

    def issue(r, c):
        for kk in range(TOP_K):
            _row_copy(h_ref, r, xs_ref, pos_ref[TOP_K * r + kk], sem).start()
        return c

    lax.fori_loop(0, tm, issue, 0)
    for _ in range(TOP_K):
        pltpu.make_async_copy(h_ref, xs_ref.at[pl.ds(0, tm), :], sem).wait()


def _dispatch(h, pos_flat, n_slots):
    n = h.shape[0]
    tm = TOKEN_TILE
    zeros = jnp.zeros((n_slots, D_MODEL), F32)
    return pl.pallas_call(
        _dispatch_kernel,
        grid=(n // tm,),
        in_specs=[
            pl.BlockSpec((TOP_K * tm,), lambda i: (i,), memory_space=pltpu.SMEM),
            pl.BlockSpec((tm, D_MODEL), lambda i: (i, 0)),
            pl.BlockSpec(memory_space=pl.ANY),
        ],
        out_specs=pl.BlockSpec(memory_space=pl.ANY),
        out_shape=jax.ShapeDtypeStruct((n_slots, D_MODEL), F32),
        scratch_shapes=[pltpu.SemaphoreType.DMA(())],
        input_output_aliases={2: 0},
        compiler_params=_cparams(("arbitrary",)),
        name="dispatch",
    )(pos_flat, h, zeros)


def _combine_kernel(pos_ref, x_ref, gate_ref, ys_ref, o_ref, buf_ref, sem):
    tm = x_ref.shape[0]

    def issue(r, c):
        for kk in range(TOP_K):
            _row_copy(ys_ref, pos_ref[TOP_K * r + kk], buf_ref.at[kk], r, sem).start()
        return c

    lax.fori_loop(0, tm, issue, 0)
    for kk in range(TOP_K):
        pltpu.make_async_copy(ys_ref.at[pl.ds(0, tm), :], buf_ref.at[kk], sem).wait()
    gate = gate_ref[...]
    out = gate[:, 0:1] * buf_ref[0] + gate[:, 1:2] * buf_ref[1]
    o_ref[...] = x_ref[...] + out


def _combine(x, gate_sorted, ys, pos_flat):
    n = x.shape[0]
    tm = TOKEN_TILE
    return pl.pallas_call(
        _combine_kernel,
        grid=(n // tm,),
        in_specs=[
            pl.BlockSpec((TOP_K * tm,), lambda i: (i,), memory_space=pltpu.SMEM),
            pl.BlockSpec((tm, D_MODEL), lambda i: (i, 0)),
            pl.BlockSpec((tm, LANES), lambda i: (i, 0)),
            pl.BlockSpec(memory_space=pl.ANY),
        ],
        out_specs=pl.BlockSpec((tm, D_MODEL), lambda i: (i, 0)),
        out_shape=jax.ShapeDtypeStruct((n, D_MODEL), F32),
        scratch_shapes=[pltpu.VMEM((TOP_K, tm, D_MODEL), F32), pltpu.SemaphoreType.DMA(())],
        compiler_params=_cparams(("arbitrary",)),
        name="combine",
    )(pos_flat, x, gate_sorted, ys)


def _moe(x, h, idx, gate, wg, wu, wd):
    n = x.shape[0]
    tm = TOKEN_TILE
    n_tiles = TOP_K * n // tm + N_EXPERTS
    n_slots = n_tiles * tm
    first = idx[:, 0] < idx[:, 1]
    e_lo = jnp.where(first, idx[:, 0], idx[:, 1])
    e_hi = jnp.where(first, idx[:, 1], idx[:, 0])
    g_lo = jnp.where(first, gate[:, 0], gate[:, 1])
    g_hi = jnp.where(first, gate[:, 1], gate[:, 0])
    experts = jnp.arange(N_EXPERTS, dtype=jnp.int32)
    sel = ((e_lo[:, None] == experts[None, :]) | (e_hi[:, None] == experts[None, :])).astype(jnp.int32)
    csum = jnp.cumsum(sel, axis=0)
    rank = csum - sel
    counts = csum[-1]
    padded = ((counts + tm - 1) // tm) * tm
    ends = jnp.cumsum(padded)
    starts = ends - padded
    slot0 = starts[None, :] + rank
    pos_lo = jnp.take_along_axis(slot0, e_lo[:, None], axis=1)[:, 0]
    pos_hi = jnp.take_along_axis(slot0, e_hi[:, None], axis=1)[:, 0]
    pos_flat = jnp.stack([pos_lo, pos_hi], axis=1).reshape(-1).astype(jnp.int32)
    tile_start = jnp.arange(n_tiles, dtype=jnp.int32) * tm
    n_valid = (ends[-1] // tm).astype(jnp.int32)
    tile_src = jnp.minimum(jnp.arange(n_tiles, dtype=jnp.int32), n_valid - 1)
    tile_expert = jnp.minimum(
        jnp.sum((tile_src[:, None] * tm >= ends[None, :]).astype(jnp.int32), axis=1), N_EXPERTS - 1).astype(jnp.int32)
    del tile_start
    gate_sorted = jnp.zeros((n, LANES), F32).at[:, 0].set(g_lo).at[:, 1].set(g_hi)
    xs = _dispatch(h, pos_flat, n_slots)
    ys = _ffn(xs, None, wg, wu, wd, tile_expert, tile_src, n_valid.reshape(1))
    return _combine(x, gate_sorted, ys, pos_flat)


def _rope_layout(w64):
    z = jnp.zeros(w64.shape[:-1] + (MLA_ROPE // 2,), w64.dtype)
    return jnp.concatenate([w64[..., :MLA_ROPE // 2], z, w64[..., MLA_ROPE // 2:], z], axis=-1)


def _qk_layout(w192):
    return jnp.concatenate([w192[..., :MLA_NOPE], _rope_layout(w192[..., MLA_NOPE:])], axis=-1)


def _layer_weights(l, norm_mix, norm_ffn, w_in, na_q_norm, na_k_norm, mla_q_a_norm, w_uq, mla_kv_a_norm, w_ukv,
                   mla_q_norm, mla_k_norm, na_out_norm, mla_out_norm, w_out):
    o5 = 3 * NA_WIDTH + MLA_Q_LORA + MLA_KV_LORA
    w_in_l = w_in[l]
    w_in_p = jnp.concatenate([w_in_l[:, :o5], _rope_layout(w_in_l[:, o5:])], axis=1)
    w_uq_p = _qk_layout(w_uq[l].reshape(MLA_Q_LORA, MLA_HEADS, MLA_QK)).reshape(MLA_Q_LORA, MLA_HEADS * MLA_QK_PAD)
    grp = jnp.arange(NA_WIDTH) // NA_HEAD_DIM
    row = lambda v: v.reshape(1, -1).astype(F32)
    return {
        "g_mix": row(norm_mix[l]),
        "g_ffn": row(norm_ffn[l]),
        "w_in": w_in_p.astype(BF16),
        "gsum": (grp[:, None] == grp[None, :]).astype(BF16),
        "g_qna": row(jnp.tile(na_q_norm[l], NA_HEADS) * (NA_HEAD_DIM ** -0.5 * LOG2E)),
        "g_kna": row(jnp.tile(na_k_norm[l], NA_HEADS)),
        "g_qa": row(mla_q_a_norm[l]),
        "w_uq": w_uq_p.astype(BF16),
        "g_kva": row(mla_kv_a_norm[l]),
        "w_ukv": w_ukv[l].astype(BF16),
        "g_q": row(_qk_layout(mla_q_norm[l]) * (MLA_QK ** -0.5 * LOG2E)),
        "g_k": row(_qk_layout(mla_k_norm[l])),
        "g_oa": row(na_out_norm[l]),
        "g_ob": row(mla_out_norm[l]),
        "w_out": w_out[l].astype(BF16),
    }


def _rope_tables(s_max):
    inv = ROPE_THETA ** (-jnp.arange(0, MLA_ROPE, 2, dtype=F32) / MLA_ROPE)
    ang = jnp.arange(s_max, dtype=F32)[:, None] * inv[None, :]
    cos, sin = jnp.cos(ang), jnp.sin(ang)
    z = jnp.zeros_like(cos)
    return jnp.concatenate([cos, z, cos, z], axis=1), jnp.concatenate([-sin, z, sin, z], axis=1)


def _trunk(x, segments, norm_mix, norm_ffn, w_in, na_q_norm, na_k_norm, na_rpb, mla_q_a_norm, w_uq, mla_kv_a_norm,
           w_ukv, mla_q_norm, mla_k_norm, na_out_norm, mla_out_norm, w_out, ffn_w_gate, ffn_w_up, ffn_w_down,
           moe_router, moe_w_gate, moe_w_up, moe_w_down):
    n = x.shape[0]
    tm = TOKEN_TILE
    depth = w_in.shape[0]
    s_max = max(s for _, s in segments)
    cos_t, sin_t = _rope_tables(s_max)
    pos_blk = jnp.asarray([i for b, s in segments for _ in range(b) for i in range(s // tm)], jnp.int32)
    n_tiles = n // tm
    dense_tiles = (jnp.zeros((n_tiles,), jnp.int32), jnp.arange(n_tiles, dtype=jnp.int32),
                   jnp.full((1,), n_tiles, jnp.int32))
    for l in range(depth):
        lw = _layer_weights(l, norm_mix, norm_ffn, w_in, na_q_norm, na_k_norm, mla_q_a_norm, w_uq, mla_kv_a_norm,
                            w_ukv, mla_q_norm, mla_k_norm, na_out_norm, mla_out_norm, w_out)
        qna, kna, vna, qm, km, vm = _proj(x, pos_blk, lw, cos_t, sin_t)
        out_a = _na(qna, kna, vna, _na_bias_table(na_rpb[l]), segments)
        out_b = jnp.zeros((n, MLA_WIDTH), F32)
        tok = 0
        for b, s in segments:
            out_b = _flash(out_b, qm, km, vm, tok, b, s)
            tok += b * s
        i = l // 2
        if l % 2 == 0:
            x, h = _merge(x, out_a, out_b, lw, route=False)
            x = _ffn(h, x, ffn_w_gate[i:i + 1].astype(BF16), ffn_w_up[i:i + 1].astype(BF16),
                     ffn_w_down[i:i + 1].astype(BF16), *dense_tiles)
        else:
            wr = jnp.pad(moe_router[i], ((0, 0), (0, LANES - N_EXPERTS)))
            lw["w_router_hi"], lw["w_router_lo"] = _split_bf16(wr)
            x, h, idx, gate = _merge(x, out_a, out_b, lw, route=True)
            x = _moe(x, h, idx[:, :TOP_K], gate[:, :TOP_K], moe_w_gate[i].astype(BF16), moe_w_up[i].astype(BF16),
                     moe_w_down[i].astype(BF16))
    return x


def kernel(x_prompt, x_sample, norm_mix, norm_ffn, w_in, na_q_norm, na_k_norm, na_rpb, mla_q_a_norm, w_uq,
           mla_kv_a_norm, w_ukv, mla_q_norm, mla_k_norm, na_out_norm, mla_out_norm, w_out, ffn_w_gate, ffn_w_up,
           ffn_w_down, moe_router, moe_w_gate, moe_w_up, moe_w_down):
    bp, sp, d = x_prompt.shape
    bs, ss, _ = x_sample.shape
    segments = ((bp, sp), (bs, ss))
    x = jnp.concatenate([x_prompt.reshape(bp * sp, d), x_sample.reshape(bs * ss, d)], axis=0)
    y = _trunk(x, segments, norm_mix, norm_ffn, w_in, na_q_norm, na_k_norm, na_rpb, mla_q_a_norm, w_uq,
               mla_kv_a_norm, w_ukv, mla_q_norm, mla_k_norm, na_out_norm, mla_out_norm, w_out, ffn_w_gate,
               ffn_w_up, ffn_w_down, moe_router, moe_w_gate, moe_w_up, moe_w_down)
    n_p = bp * sp
    return y[:n_p].reshape(bp, sp, d), y[n_p:].reshape(bs, ss, d)
```

```python
import functools
import math

import jax
import jax.numpy as jnp
import numpy as np
from jax import lax
from jax.experimental import pallas as pl
from jax.experimental.pallas import tpu as pltpu

F32 = jnp.float32
BF16 = jnp.bfloat16

D_MODEL = 1024
GRID_W = 64
NA_HEADS = 8
NA_HEAD_DIM = 64
NA_ROWS = 8
NA_COLS = 16
NA_WIDTH = NA_HEADS * NA_HEAD_DIM
MLA_HEADS = 4
MLA_Q_LORA = 384
MLA_KV_LORA = 256
MLA_NOPE = 128
MLA_ROPE = 64
MLA_V = 128
MLA_QK = MLA_NOPE + MLA_ROPE
MLA_WIDTH = MLA_HEADS * MLA_V
MLA_QK_PAD = 256
ROPE_THETA = 10000.0
N_EXPERTS = 8
TOP_K = 2
EPS = 1e-6
NEG_INF = -1e30
LOG2E = math.log2(math.e)

LANES = 128
TOKEN_TILE = 512
NA_BLOCK_ROWS = 8
NA_WINDOW_ROWS = 16
NA_ROW_UNROLL = 4
FLASH_TQ = 1024
FLASH_TK = TOKEN_TILE
FLASH_CHUNKS_PER_TRIP = 8
ROW_DMA_UNROLL = 8
FFN_CHUNK = 1408
VMEM_LIMIT = 56 * 1024 * 1024


def _cparams(sem):
    return pltpu.CompilerParams(dimension_semantics=sem, vmem_limit_bytes=VMEM_LIMIT)


def _const_spec(shape):
    nd = len(shape)
    return pl.BlockSpec(shape, lambda *_: (0,) * nd, pipeline_mode=pl.Buffered(1))


def _rms(xf, g):
    y = xf * lax.rsqrt(jnp.mean(xf * xf, axis=-1, keepdims=True) + EPS)
    return y * g


def _split_bf16(v):
    hi = v.astype(BF16)
    lo = (v - hi.astype(F32)).astype(BF16)
    return hi, lo


def _proj_kernel(pos_ref, x_ref, gmix_ref, win_ref, gsum_ref, gqna_ref, gkna_ref, gqa_ref, wuq_ref,
                 gkva_ref, wukv_ref, gq_ref, gk_ref, cos_ref, sin_ref,
                 qna_ref, kna_ref, vna_ref, qm_ref, km_ref, vm_ref):
    del pos_ref
    h = _rms(x_ref[...], gmix_ref[...]).astype(BF16)

    def proj(lo, hi):
        return jnp.dot(h, win_ref[:, lo:hi], preferred_element_type=F32)

    gsum = gsum_ref[...]

    def head_norm(v, gain):
        ss = jnp.dot((v * v).astype(BF16), gsum, preferred_element_type=F32)
        return (v * lax.rsqrt(ss * (1.0 / NA_HEAD_DIM) + EPS)) * gain

    o1, o2, o3 = NA_WIDTH, 2 * NA_WIDTH, 3 * NA_WIDTH
    o4 = o3 + MLA_Q_LORA
    o5 = o4 + MLA_KV_LORA
    qna_ref[...] = head_norm(proj(0, o1), gqna_ref[...]).astype(BF16)
    kna_ref[...] = head_norm(proj(o1, o2), gkna_ref[...]).astype(BF16)
    vna_ref[...] = proj(o2, o3).astype(BF16)

    cos = cos_ref[...]
    sin = sin_ref[...]

    def rope(u):
        return u * cos + pltpu.roll(u, 64, 1) * sin

    cq = _rms(proj(o3, o4), gqa_ref[...]).astype(BF16)
    q_all = jnp.dot(cq, wuq_ref[...], preferred_element_type=F32)
    gq = gq_ref[...]
    for hd in range(MLA_HEADS):
        qh = q_all[:, hd * MLA_QK_PAD:(hd + 1) * MLA_QK_PAD]
        r = lax.rsqrt(jnp.sum(qh * qh, axis=-1, keepdims=True) * (1.0 / MLA_QK) + EPS)
        qn = (qh * r) * gq
        qm_ref[hd, 0:MLA_NOPE, :] = qn[:, 0:MLA_NOPE].T.astype(BF16)
        qm_ref[hd, MLA_NOPE:MLA_QK_PAD, :] = rope(qn[:, MLA_NOPE:MLA_QK_PAD]).T.astype(BF16)

    ckv = _rms(proj(o4, o5), gkva_ref[...]).astype(BF16)
    kv = jnp.dot(ckv, wukv_ref[...], preferred_element_type=F32)
    kpe = proj(o5, o5 + LANES)
    ss_pe = jnp.sum(kpe * kpe, axis=-1, keepdims=True)
    gk = gk_ref[...]
    for hd in range(MLA_HEADS):
        base = hd * (MLA_NOPE + MLA_V)
        kn = kv[:, base:base + MLA_NOPE]
        r = lax.rsqrt((jnp.sum(kn * kn, axis=-1, keepdims=True) + ss_pe) * (1.0 / MLA_QK) + EPS)
        km_ref[hd, :, 0:MLA_NOPE] = ((kn * r) * gk[:, 0:MLA_NOPE]).astype(BF16)
        km_ref[hd, :, MLA_NOPE:MLA_QK_PAD] = rope((kpe * r) * gk[:, MLA_NOPE:MLA_QK_PAD]).astype(BF16)
        vm_ref[hd, 0] = kv[:, base + MLA_NOPE:base + MLA_NOPE + MLA_V].T.astype(BF16)


def _proj(x, pos_blk, lw, cos_t, sin_t):
    n = x.shape[0]
    tm = TOKEN_TILE
    row = lambda i, pos: (i, 0)
    head = lambda i, pos: (0, i, 0)
    in_specs = [
        pl.BlockSpec((tm, D_MODEL), row),
        _const_spec((1, D_MODEL)),
        _const_spec(lw["w_in"].shape),
        _const_spec((NA_WIDTH, NA_WIDTH)),
        _const_spec((1, NA_WIDTH)),
        _const_spec((1, NA_WIDTH)),
        _const_spec((1, MLA_Q_LORA)),
        _const_spec(lw["w_uq"].shape),
        _const_spec((1, MLA_KV_LORA)),
        _const_spec(lw["w_ukv"].shape),
        _const_spec((1, MLA_QK_PAD)),
        _const_spec((1, MLA_QK_PAD)),
        pl.BlockSpec((tm, LANES), lambda i, pos: (pos[i], 0)),
        pl.BlockSpec((tm, LANES), lambda i, pos: (pos[i], 0)),
    ]
    out_specs = [
        pl.BlockSpec((tm, NA_WIDTH), row),
        pl.BlockSpec((tm, NA_WIDTH), row),
        pl.BlockSpec((tm, NA_WIDTH), row),
        pl.BlockSpec((MLA_HEADS, MLA_QK_PAD, tm), lambda i, pos: (0, 0, i)),
        pl.BlockSpec((MLA_HEADS, tm, MLA_QK_PAD), head),
        pl.BlockSpec((MLA_HEADS, 1, MLA_V, tm), lambda i, pos: (0, i, 0, 0)),
    ]
    out_shape = [
        jax.ShapeDtypeStruct((n, NA_WIDTH), BF16),
        jax.ShapeDtypeStruct((n, NA_WIDTH), BF16),
        jax.ShapeDtypeStruct((n, NA_WIDTH), BF16),
        jax.ShapeDtypeStruct((MLA_HEADS, MLA_QK_PAD, n), BF16),
        jax.ShapeDtypeStruct((MLA_HEADS, n, MLA_QK_PAD), BF16),
        jax.ShapeDtypeStruct((MLA_HEADS, n // tm, MLA_V, tm), BF16),
    ]
    return pl.pallas_call(
        _proj_kernel,
        grid_spec=pltpu.PrefetchScalarGridSpec(
            num_scalar_prefetch=1, grid=(n // tm,), in_specs=in_specs, out_specs=out_specs),
        out_shape=out_shape,
        compiler_params=_cparams(("arbitrary",)),
        name="proj",
    )(pos_blk, x, lw["g_mix"], lw["w_in"], lw["gsum"], lw["g_qna"], lw["g_kna"], lw["g_qa"], lw["w_uq"],
      lw["g_kva"], lw["w_ukv"], lw["g_q"], lw["g_k"], cos_t, sin_t)


def _na_kernel(w0_ref, kind_ref, q_ref, k_ref, v_ref, bias_ref, o_ref):
    del w0_ref
    kind = kind_ref[pl.program_id(0)]
    is_first = kind == 0
    is_last = kind == 2
    lane = lax.broadcasted_iota(jnp.int32, (GRID_W, LANES), 1)
    lo_half = lane < NA_HEAD_DIM
    half_rows = NA_ROWS // 2

    def one_row(j, carry):
        sh = j - half_rows
        off = jnp.where(is_first, jnp.maximum(sh, 0), jnp.where(is_last, NA_BLOCK_ROWS + jnp.minimum(sh, 0), j))
        dcls = jnp.where(is_first, jnp.minimum(j, half_rows), jnp.where(is_last, jnp.maximum(j, half_rows), half_rows))
        kstart = pl.multiple_of(off * GRID_W, GRID_W)
        qstart = pl.multiple_of(j * GRID_W, GRID_W)
        scores = []
        for p in range(NA_HEADS // 2):
            cols = slice(p * LANES, (p + 1) * LANES)
            qp = q_ref[pl.ds(qstart, GRID_W), cols]
            kp = k_ref[pl.ds(kstart, NA_ROWS * GRID_W), cols]
            zero = jnp.zeros_like(qp)
            q2 = jnp.concatenate([jnp.where(lo_half, qp, zero), jnp.where(lo_half, zero, qp)], axis=0)
            s = lax.dot_general(q2, kp, (((1,), (1,)), ((), ())), preferred_element_type=F32)
            scores.append(s + bias_ref[dcls, p])
        s = jnp.concatenate(scores, axis=0)
        e = jnp.exp2(s - jnp.max(s, axis=-1, keepdims=True))
        inv_l = 1.0 / jnp.sum(e, axis=-1, keepdims=True)
        pb = e.astype(BF16)
        for p in range(NA_HEADS // 2):
            cols = slice(p * LANES, (p + 1) * LANES)
            rows = slice(2 * p * GRID_W, (2 * p + 2) * GRID_W)
            vp = v_ref[pl.ds(kstart, NA_ROWS * GRID_W), cols]
            o2 = jnp.dot(pb[rows], vp, preferred_element_type=F32) * inv_l[rows]
            o_ref[pl.ds(qstart, GRID_W), cols] = jnp.where(lo_half, o2[0:GRID_W], o2[GRID_W:2 * GRID_W])
        return carry

    lax.fori_loop(0, NA_BLOCK_ROWS, one_row, 0, unroll=NA_ROW_UNROLL)


def _na_bias_table(rpb):
    c = np.arange(GRID_W)
    cs = np.clip(c - NA_COLS // 2, 0, GRID_W - NA_COLS)
    kc = np.arange(GRID_W)
    valid = (kc[None, :] >= cs[:, None]) & (kc[None, :] < cs[:, None] + NA_COLS)
    dc = kc[None, :] - c[:, None] + (NA_COLS - 1)
    onehot = ((np.arange(2 * NA_COLS - 1)[:, None, None] == dc[None]) & valid[None]).astype(np.float32)
    t = jnp.einsum("hrd,dck->hrck", rpb.astype(F32), jnp.asarray(onehot), precision=lax.Precision.HIGHEST)
    t = jnp.where(jnp.asarray(valid)[None, None], t * LOG2E, NEG_INF)
    tab = jnp.stack([t[:, NA_ROWS - 1 - d:2 * NA_ROWS - 1 - d] for d in range(NA_ROWS)], axis=0)
    tab = jnp.transpose(tab, (0, 1, 3, 2, 4))
    return tab.reshape(NA_ROWS, NA_HEADS // 2, 2 * GRID_W, NA_ROWS * GRID_W)


def _na_block_tables(segments):
    w0, kind = [], []
    tok = 0
    for b, s in segments:
        rows = s // GRID_W
        nb = rows // NA_BLOCK_ROWS
        for _ in range(b):
            for i in range(nb):
                r0 = min(max(NA_BLOCK_ROWS * i - NA_ROWS // 2, 0), rows - NA_WINDOW_ROWS)
                w0.append(tok // GRID_W + r0)
                kind.append(0 if i == 0 else (2 if i == nb - 1 else 1))
            tok += s
    return jnp.asarray(w0, jnp.int32), jnp.asarray(kind, jnp.int32)


def _na(q, k, v, bias, segments):
    n = q.shape[0]
    tq = NA_BLOCK_ROWS * GRID_W
    tw = NA_WINDOW_ROWS * GRID_W
    w0, kind = _na_block_tables(segments)
    win = pl.BlockSpec((pl.Element(tw), pl.Element(NA_WIDTH)), lambda i, w0, kind: (w0[i] * GRID_W, 0))
    return pl.pallas_call(
        _na_kernel,
        grid_spec=pltpu.PrefetchScalarGridSpec(
            num_scalar_prefetch=2,
            grid=(n // tq,),
            in_specs=[
                pl.BlockSpec((tq, NA_WIDTH), lambda i, w0, kind: (i, 0)),
                win,
                win,
                pl.BlockSpec(bias.shape, lambda i, w0, kind: (0, 0, 0, 0), pipeline_mode=pl.Buffered(1)),
            ],
            out_specs=pl.BlockSpec((tq, NA_WIDTH), lambda i, w0, kind: (i, 0)),
        ),
        out_shape=jax.ShapeDtypeStruct((n, NA_WIDTH), F32),
        compiler_params=_cparams(("arbitrary",)),
        name="na",
    )(w0, kind, q, k, v, bias)


def _flash_kernel(prev_ref, qt_ref, k_ref, vt_ref, o_ref, s_scr, acc_ref, *, n_chunks, tk, per_trip):
    del prev_ref
    acc_ref[...] = jnp.zeros(acc_ref.shape, F32)
    qt = qt_ref[...]
    tq = qt.shape[1]

    def scores(c, slot):
        start = pl.multiple_of(c * tk, tk)
        s_scr[slot] = jnp.dot(k_ref[pl.ds(start, tk), :], qt, preferred_element_type=F32)

    def softmax_pv(c, slot, m_prev, l_prev):
        st = s_scr[slot]
        m_new = jnp.maximum(m_prev, jnp.max(st, axis=0, keepdims=True))
        alpha = jnp.exp2(m_prev - m_new)
        pt = jnp.exp2(st - m_new)
        l_new = alpha * l_prev + jnp.sum(pt, axis=0, keepdims=True)
        acc_ref[...] = alpha * acc_ref[...] + jnp.dot(vt_ref[c], pt.astype(BF16), preferred_element_type=F32)
        return m_new, l_new

    scores(0, 0)

    def body(i, carry):
        m, l = carry
        for u in range(per_trip // 2):
            c0 = per_trip * i + 2 * u
            scores(c0 + 1, 1)
            m, l = softmax_pv(c0, 0, m, l)
            scores(jnp.minimum(c0 + 2, n_chunks - 1), 0)
            m, l = softmax_pv(c0 + 1, 1, m, l)
        return m, l

    init = (jnp.full((1, tq), -jnp.inf, F32), jnp.zeros((1, tq), F32))
    _, l = lax.fori_loop(0, n_chunks // per_trip, body, init)
    o_ref[...] = (acc_ref[...] / l).T


def _flash(prev, qt, k, vt, tok_off, b, s):
    n = k.shape[1]
    tq, tk = FLASH_TQ, FLASH_TK
    assert tok_off % s == 0 and s % tq == 0 and s % (2 * tk) == 0 and vt.shape[-1] == tk
    nq = s // tq
    nc = s // tk
    per_trip = math.gcd(nc, FLASH_CHUNKS_PER_TRIP)
    qblk0 = tok_off // tq
    sblk0 = tok_off // s
    return pl.pallas_call(
        functools.partial(_flash_kernel, n_chunks=nc, tk=tk, per_trip=per_trip),
        grid=(b, MLA_HEADS, nq),
        in_specs=[
            pl.BlockSpec(memory_space=pl.ANY),
            pl.BlockSpec((None, MLA_QK_PAD, tq), lambda bi, h, i: (h, 0, qblk0 + bi * nq + i)),
            pl.BlockSpec((None, s, MLA_QK_PAD), lambda bi, h, i: (h, sblk0 + bi, 0)),
            pl.BlockSpec((None, nc, MLA_V, tk), lambda bi, h, i: (h, sblk0 + bi, 0, 0)),
        ],
        out_specs=pl.BlockSpec((tq, MLA_V), lambda bi, h, i: (qblk0 + bi * nq + i, h)),
        out_shape=jax.ShapeDtypeStruct((n, MLA_WIDTH), F32),
        scratch_shapes=[pltpu.VMEM((2, tk, tq), F32), pltpu.VMEM((MLA_V, tq), F32)],
        input_output_aliases={0: 0},
        compiler_params=_cparams(("arbitrary", "arbitrary", "arbitrary")),
        name="flash",
    )(prev, qt, k, vt)


def _merge_kernel(*refs, route):
    if route:
        (x_ref, a_ref, b_ref, ga_ref, gb_ref, wout_ref, gffn_ref, wrh_ref, wrl_ref,
         xo_ref, h_ref, idx_ref, gate_ref) = refs
    else:
        x_ref, a_ref, b_ref, ga_ref, gb_ref, wout_ref, gffn_ref, xo_ref, h_ref = refs
    a = _rms(a_ref[...], ga_ref[...]).astype(BF16)
    b = _rms(b_ref[...], gb_ref[...]).astype(BF16)
    y = (jnp.dot(a, wout_ref[0:NA_WIDTH, :], preferred_element_type=F32)
         + jnp.dot(b, wout_ref[NA_WIDTH:NA_WIDTH + MLA_WIDTH, :], preferred_element_type=F32))
    xn = x_ref[...] + y
    xo_ref[...] = xn
    hf = _rms(xn, gffn_ref[...])
    h_ref[...] = hf.astype(h_ref.dtype)
    if route:
        hh, hl = _split_bf16(hf)
        wh = wrh_ref[...]
        logits = (jnp.dot(hh, wh, preferred_element_type=F32)
                  + (jnp.dot(hh, wrl_ref[...], preferred_element_type=F32)
                     + jnp.dot(hl, wh, preferred_element_type=F32)))
        lane = lax.broadcasted_iota(jnp.int32, logits.shape, 1)
        lg = jnp.where(lane < N_EXPERTS, logits, -jnp.inf)
        t1 = jnp.max(lg, axis=-1, keepdims=True)
        i1 = jnp.min(jnp.where(lg == t1, lane, LANES), axis=-1, keepdims=True)
        lg2 = jnp.where(lane == i1, -jnp.inf, lg)
        t2 = jnp.max(lg2, axis=-1, keepdims=True)
        i2 = jnp.min(jnp.where(lg2 == t2, lane, LANES), axis=-1, keepdims=True)
        e2 = jnp.exp(t2 - t1)
        den = 1.0 + e2
        idx_ref[...] = jnp.where(lane == 0, i1, jnp.where(lane == 1, i2, 0))
        gate_ref[...] = jnp.where(lane == 0, 1.0 / den, jnp.where(lane == 1, e2 / den, 0.0))


def _merge(x, out_a, out_b, lw, route):
    n = x.shape[0]
    tm = TOKEN_TILE
    row = lambda i: (i, 0)
    in_specs = [
        pl.BlockSpec((tm, D_MODEL), row),
        pl.BlockSpec((tm, NA_WIDTH), row),
        pl.BlockSpec((tm, MLA_WIDTH), row),
        _const_spec((1, NA_WIDTH)),
        _const_spec((1, MLA_WIDTH)),
        _const_spec((NA_WIDTH + MLA_WIDTH, D_MODEL)),
        _const_spec((1, D_MODEL)),
    ]
    args = [x, out_a, out_b, lw["g_oa"], lw["g_ob"], lw["w_out"], lw["g_ffn"]]
    out_specs = [pl.BlockSpec((tm, D_MODEL), row), pl.BlockSpec((tm, D_MODEL), row)]
    out_shape = [jax.ShapeDtypeStruct((n, D_MODEL), F32),
                 jax.ShapeDtypeStruct((n, D_MODEL), F32 if route else BF16)]
    if route:
        in_specs += [_const_spec((D_MODEL, LANES)), _const_spec((D_MODEL, LANES))]
        args += [lw["w_router_hi"], lw["w_router_lo"]]
        out_specs += [pl.BlockSpec((tm, LANES), row), pl.BlockSpec((tm, LANES), row)]
        out_shape += [jax.ShapeDtypeStruct((n, LANES), jnp.int32), jax.ShapeDtypeStruct((n, LANES), F32)]
    return pl.pallas_call(
        functools.partial(_merge_kernel, route=route),
        grid=(n // tm,),
        in_specs=in_specs,
        out_specs=out_specs,
        out_shape=out_shape,
        compiler_params=_cparams(("arbitrary",)),
        name="merge_route" if route else "merge",
    )(*args)


def _ffn_kernel(*refs, residual):
    if residual:
        te_ref, src_ref, nv_ref, h_ref, x_ref, wg_ref, wu_ref, wd_ref, o_ref = refs
    else:
        te_ref, src_ref, nv_ref, h_ref, wg_ref, wu_ref, wd_ref, o_ref = refs
    del te_ref, src_ref
    d_ff = wg_ref.shape[-1]

    @pl.when(pl.program_id(0) < nv_ref[0])
    def _():
        h = h_ref[...].astype(BF16)
        acc = x_ref[...] if residual else None
        for c in range(d_ff // FFN_CHUNK):
            sl = slice(c * FFN_CHUNK, (c + 1) * FFN_CHUNK)
            g = jnp.dot(h, wg_ref[:, sl], preferred_element_type=F32)
            u = jnp.dot(h, wu_ref[:, sl], preferred_element_type=F32)
            act = ((g * jax.nn.sigmoid(g)) * u).astype(BF16)
            y = jnp.dot(act, wd_ref[sl, :], preferred_element_type=F32)
            acc = y if acc is None else acc + y
        o_ref[...] = acc


def _ffn(h, x, wg, wu, wd, tile_expert, tile_src, n_valid):
    n = h.shape[0]
    tm = TOKEN_TILE
    d_ff = wg.shape[-1]
    assert d_ff % FFN_CHUNK == 0
    row = lambda i, te, src, nv: (src[i], 0)
    wspec = lambda shp: pl.BlockSpec((None,) + shp, lambda i, te, src, nv: (te[i], 0, 0),
                                     pipeline_mode=pl.Buffered(1))
    residual = x is not None
    in_specs = [pl.BlockSpec((tm, D_MODEL), row)]
    args = [h]
    if residual:
        in_specs.append(pl.BlockSpec((tm, D_MODEL), row))
        args.append(x)
    in_specs += [wspec((D_MODEL, d_ff)), wspec((D_MODEL, d_ff)), wspec((d_ff, D_MODEL))]
    args += [wg, wu, wd]
    return pl.pallas_call(
        functools.partial(_ffn_kernel, residual=residual),
        grid_spec=pltpu.PrefetchScalarGridSpec(
            num_scalar_prefetch=3, grid=(n // tm,), in_specs=in_specs,
            out_specs=pl.BlockSpec((tm, D_MODEL), row)),
        out_shape=jax.ShapeDtypeStruct((n, D_MODEL), F32),
        compiler_params=_cparams(("arbitrary",)),
        name="ffn" if residual else "expert_ffn",
    )(tile_expert, tile_src, n_valid, *args)


def _row_copy(src_ref, src_row, dst_ref, dst_row, sem):
    return pltpu.make_async_copy(src_ref.at[pl.ds(src_row, 1), :], dst_ref.at[pl.ds(dst_row, 1), :], sem)


def _dispatch_kernel(pos_ref, h_ref, zero_ref, xs_ref, sem):
    del zero_ref
    tm = h_ref.shape[0]

    def issue(r, c):
        for kk in range(TOP_K):
            _row_copy(h_ref, r, xs_ref, pos_ref[TOP_K * r + kk], sem).start()
        return c

    lax.fori_loop(0, tm, issue, 0, unroll=ROW_DMA_UNROLL)
    for _ in range(TOP_K):
        pltpu.make_async_copy(h_ref, xs_ref.at[pl.ds(0, tm), :], sem).wait()


def _dispatch(h, pos_flat, n_slots):
    n = h.shape[0]
    tm = TOKEN_TILE
    zeros = jnp.zeros((n_slots, D_MODEL), F32)
    return pl.pallas_call(
        _dispatch_kernel,
        grid=(n // tm,),
        in_specs=[
            pl.BlockSpec((TOP_K * tm,), lambda i: (i,), memory_space=pltpu.SMEM),
            pl.BlockSpec((tm, D_MODEL), lambda i: (i, 0)),
            pl.BlockSpec(memory_space=pl.ANY),
        ],
        out_specs=pl.BlockSpec(memory_space=pl.ANY),
        out_shape=jax.ShapeDtypeStruct((n_slots, D_MODEL), F32),
        scratch_shapes=[pltpu.SemaphoreType.DMA(())],
        input_output_aliases={2: 0},
        compiler_params=_cparams(("arbitrary",)),
        name="dispatch",
    )(pos_flat, h, zeros)


def _combine_kernel(pos_ref, x_ref, gate_ref, ys_ref, o_ref, buf_ref, sem):
    tm = x_ref.shape[0]

    def issue(r, c):
        for kk in range(TOP_K):
            _row_copy(ys_ref, pos_ref[TOP_K * r + kk], buf_ref.at[kk], r, sem).start()
        return c

    lax.fori_loop(0, tm, issue, 0, unroll=ROW_DMA_UNROLL)
    for kk in range(TOP_K):
        pltpu.make_async_copy(ys_ref.at[pl.ds(0, tm), :], buf_ref.at[kk], sem).wait()
    gate = gate_ref[...]
    out = gate[:, 0:1] * buf_ref[0] + gate[:, 1:2] * buf_ref[1]
    o_ref[...] = x_ref[...] + out


def _combine(x, gate_sorted, ys, pos_flat):
    n = x.shape[0]
    tm = TOKEN_TILE
    return pl.pallas_call(
        _combine_kernel,
        grid=(n // tm,),
        in_specs=[
            pl.BlockSpec((TOP_K * tm,), lambda i: (i,), memory_space=pltpu.SMEM),
            pl.BlockSpec((tm, D_MODEL), lambda i: (i, 0)),
            pl.BlockSpec((tm, LANES), lambda i: (i, 0)),
            pl.BlockSpec(memory_space=pl.ANY),
        ],
        out_specs=pl.BlockSpec((tm, D_MODEL), lambda i: (i, 0)),
        out_shape=jax.ShapeDtypeStruct((n, D_MODEL), F32),
        scratch_shapes=[pltpu.VMEM((TOP_K, tm, D_MODEL), F32), pltpu.SemaphoreType.DMA(())],
        compiler_params=_cparams(("arbitrary",)),
        name="combine",
    )(pos_flat, x, gate_sorted, ys)


def _moe(x, h, idx, gate, wg, wu, wd):
    n = x.shape[0]
    tm = TOKEN_TILE
    n_tiles = TOP_K * n // tm + N_EXPERTS
    n_slots = n_tiles * tm
    first = idx[:, 0] < idx[:, 1]
    e_lo = jnp.where(first, idx[:, 0], idx[:, 1])
    e_hi = jnp.where(first, idx[:, 1], idx[:, 0])
    g_lo = jnp.where(first, gate[:, 0], gate[:, 1])
    g_hi = jnp.where(first, gate[:, 1], gate[:, 0])
    experts = jnp.arange(N_EXPERTS, dtype=jnp.int32)
    sel = ((e_lo[:, None] == experts[None, :]) | (e_hi[:, None] == experts[None, :])).astype(jnp.int32)
    csum = jnp.cumsum(sel, axis=0)
    rank = csum - sel
    counts = csum[-1]
    padded = ((counts + tm - 1) // tm) * tm
    ends = jnp.cumsum(padded)
    starts = ends - padded
    slot0 = starts[None, :] + rank
    pos_lo = jnp.take_along_axis(slot0, e_lo[:, None], axis=1)[:, 0]
    pos_hi = jnp.take_along_axis(slot0, e_hi[:, None], axis=1)[:, 0]
    pos_flat = jnp.stack([pos_lo, pos_hi], axis=1).reshape(-1).astype(jnp.int32)
    tile_start = jnp.arange(n_tiles, dtype=jnp.int32) * tm
    n_valid = (ends[-1] // tm).astype(jnp.int32)
    tile_src = jnp.minimum(jnp.arange(n_tiles, dtype=jnp.int32), n_valid - 1)
    tile_expert = jnp.minimum(
        jnp.sum((tile_src[:, None] * tm >= ends[None, :]).astype(jnp.int32), axis=1), N_EXPERTS - 1).astype(jnp.int32)
    del tile_start
    gate_sorted = jnp.zeros((n, LANES), F32).at[:, 0].set(g_lo).at[:, 1].set(g_hi)
    xs = _dispatch(h, pos_flat, n_slots)
    ys = _ffn(xs, None, wg, wu, wd, tile_expert, tile_src, n_valid.reshape(1))
    return _combine(x, gate_sorted, ys, pos_flat)


def _rope_layout(w64):
    z = jnp.zeros(w64.shape[:-1] + (MLA_ROPE // 2,), w64.dtype)
    return jnp.concatenate([w64[..., :MLA_ROPE // 2], z, w64[..., MLA_ROPE // 2:], z], axis=-1)


def _qk_layout(w192):
    return jnp.concatenate([w192[..., :MLA_NOPE], _rope_layout(w192[..., MLA_NOPE:])], axis=-1)


def _layer_weights(l, norm_mix, norm_ffn, w_in, na_q_norm, na_k_norm, mla_q_a_norm, w_uq, mla_kv_a_norm, w_ukv,
                   mla_q_norm, mla_k_norm, na_out_norm, mla_out_norm, w_out):
    o5 = 3 * NA_WIDTH + MLA_Q_LORA + MLA_KV_LORA
    w_in_l = w_in[l]
    w_in_p = jnp.concatenate([w_in_l[:, :o5], _rope_layout(w_in_l[:, o5:])], axis=1)
    w_uq_p = _qk_layout(w_uq[l].reshape(MLA_Q_LORA, MLA_HEADS, MLA_QK)).reshape(MLA_Q_LORA, MLA_HEADS * MLA_QK_PAD)
    grp = jnp.arange(NA_WIDTH) // NA_HEAD_DIM
    row = lambda v: v.reshape(1, -1).astype(F32)
    return {
        "g_mix": row(norm_mix[l]),
        "g_ffn": row(norm_ffn[l]),
        "w_in": w_in_p.astype(BF16),
        "gsum": (grp[:, None] == grp[None, :]).astype(BF16),
        "g_qna": row(jnp.tile(na_q_norm[l], NA_HEADS) * (NA_HEAD_DIM ** -0.5 * LOG2E)),
        "g_kna": row(jnp.tile(na_k_norm[l], NA_HEADS)),
        "g_qa": row(mla_q_a_norm[l]),
        "w_uq": w_uq_p.astype(BF16),
        "g_kva": row(mla_kv_a_norm[l]),
        "w_ukv": w_ukv[l].astype(BF16),
        "g_q": row(_qk_layout(mla_q_norm[l]) * (MLA_QK ** -0.5 * LOG2E)),
        "g_k": row(_qk_layout(mla_k_norm[l])),
        "g_oa": row(na_out_norm[l]),
        "g_ob": row(mla_out_norm[l]),
        "w_out": w_out[l].astype(BF16),
    }


def _rope_tables(s_max):
    inv = ROPE_THETA ** (-jnp.arange(0, MLA_ROPE, 2, dtype=F32) / MLA_ROPE)
    ang = jnp.arange(s_max, dtype=F32)[:, None] * inv[None, :]
    cos, sin = jnp.cos(ang), jnp.sin(ang)
    z = jnp.zeros_like(cos)
    return jnp.concatenate([cos, z, cos, z], axis=1), jnp.concatenate([-sin, z, sin, z], axis=1)


def _trunk(x, segments, norm_mix, norm_ffn, w_in, na_q_norm, na_k_norm, na_rpb, mla_q_a_norm, w_uq, mla_kv_a_norm,
           w_ukv, mla_q_norm, mla_k_norm, na_out_norm, mla_out_norm, w_out, ffn_w_gate, ffn_w_up, ffn_w_down,
           moe_router, moe_w_gate, moe_w_up, moe_w_down):
    n = x.shape[0]
    tm = TOKEN_TILE
    depth = w_in.shape[0]
    s_max = max(s for _, s in segments)
    cos_t, sin_t = _rope_tables(s_max)
    pos_blk = jnp.asarray([i for b, s in segments for _ in range(b) for i in range(s // tm)], jnp.int32)
    n_tiles = n // tm
    dense_tiles = (jnp.zeros((n_tiles,), jnp.int32), jnp.arange(n_tiles, dtype=jnp.int32),
                   jnp.full((1,), n_tiles, jnp.int32))
    for l in range(depth):
        lw = _layer_weights(l, norm_mix, norm_ffn, w_in, na_q_norm, na_k_norm, mla_q_a_norm, w_uq, mla_kv_a_norm,
                            w_ukv, mla_q_norm, mla_k_norm, na_out_norm, mla_out_norm, w_out)
        qna, kna, vna, qm, km, vm = _proj(x, pos_blk, lw, cos_t, sin_t)
        out_a = _na(qna, kna, vna, _na_bias_table(na_rpb[l]), segments)
        out_b = jnp.zeros((n, MLA_WIDTH), F32)
        tok = 0
        for b, s in segments:
            out_b = _flash(out_b, qm, km, vm, tok, b, s)
            tok += b * s
        i = l // 2
        if l % 2 == 0:
            x, h = _merge(x, out_a, out_b, lw, route=False)
            x = _ffn(h, x, ffn_w_gate[i:i + 1].astype(BF16), ffn_w_up[i:i + 1].astype(BF16),
                     ffn_w_down[i:i + 1].astype(BF16), *dense_tiles)
        else:
            wr = jnp.pad(moe_router[i], ((0, 0), (0, LANES - N_EXPERTS)))
            lw["w_router_hi"], lw["w_router_lo"] = _split_bf16(wr)
            x, h, idx, gate = _merge(x, out_a, out_b, lw, route=True)
            x = _moe(x, h, idx[:, :TOP_K], gate[:, :TOP_K], moe_w_gate[i].astype(BF16), moe_w_up[i].astype(BF16),
                     moe_w_down[i].astype(BF16))
    return x


def kernel(x_prompt, x_sample, norm_mix, norm_ffn, w_in, na_q_norm, na_k_norm, na_rpb, mla_q_a_norm, w_uq,
           mla_kv_a_norm, w_ukv, mla_q_norm, mla_k_norm, na_out_norm, mla_out_norm, w_out, ffn_w_gate, ffn_w_up,
           ffn_w_down, moe_router, moe_w_gate, moe_w_up, moe_w_down):
    bp, sp, d = x_prompt.shape
    bs, ss, _ = x_sample.shape
    segments = ((bp, sp), (bs, ss))
    x = jnp.concatenate([x_prompt.reshape(bp * sp, d), x_sample.reshape(bs * ss, d)], axis=0)
    y = _trunk(x, segments, norm_mix, norm_ffn, w_in, na_q_norm, na_k_norm, na_rpb, mla_q_a_norm, w_uq,
               mla_kv_a_norm, w_ukv, mla_q_norm, mla_k_norm, na_out_norm, mla_out_norm, w_out, ffn_w_gate,
               ffn_w_up, ffn_w_down, moe_router, moe_w_gate, moe_w_up, moe_w_down)
    n_p = bp * sp
    return y[:n_p].reshape(bp, sp, d), y[n_p:].reshape(bs, ss, d)
```

```python
import functools
import math

import jax
import jax.numpy as jnp
import numpy as np
from jax import lax
from jax.experimental import pallas as pl
from jax.experimental.pallas import tpu as pltpu

F32 = jnp.float32
BF16 = jnp.bfloat16

D_MODEL = 1024
GRID_W = 64
NA_HEADS = 8
NA_HEAD_DIM = 64
NA_ROWS = 8
NA_COLS = 16
NA_WIDTH = NA_HEADS * NA_HEAD_DIM
MLA_HEADS = 4
MLA_Q_LORA = 384
MLA_KV_LORA = 256
MLA_NOPE = 128
MLA_ROPE = 64
MLA_V = 128
MLA_QK = MLA_NOPE + MLA_ROPE
MLA_WIDTH = MLA_HEADS * MLA_V
MLA_QK_PAD = 256
ROPE_THETA = 10000.0
N_EXPERTS = 8
TOP_K = 2
EPS = 1e-6
NEG_INF = -1e30
LOG2E = math.log2(math.e)

LANES = 128
TOKEN_TILE = 512
NA_BLOCK_ROWS = 8
NA_WINDOW_ROWS = 16
NA_ROW_UNROLL = 4
FLASH_TQ = 1024
FLASH_TK = TOKEN_TILE
FLASH_CHUNKS_PER_TRIP = 8
ROW_DMA_UNROLL = 8
FFN_CHUNK = 1408
VMEM_LIMIT = 56 * 1024 * 1024


def _cparams(sem):
    return pltpu.CompilerParams(dimension_semantics=sem, vmem_limit_bytes=VMEM_LIMIT)


def _const_spec(shape):
    nd = len(shape)
    return pl.BlockSpec(shape, lambda *_: (0,) * nd, pipeline_mode=pl.Buffered(1))


def _rms(xf, g):
    y = xf * lax.rsqrt(jnp.mean(xf * xf, axis=-1, keepdims=True) + EPS)
    return y * g


def _split_bf16(v):
    hi = v.astype(BF16)
    lo = (v - hi.astype(F32)).astype(BF16)
    return hi, lo


def _proj_kernel(pos_ref, x_ref, gmix_ref, win_ref, gsum_ref, gqna_ref, gkna_ref, gqa_ref, wuq_ref,
                 gkva_ref, wukv_ref, gq_ref, gk_ref, cos_ref, sin_ref,
                 qna_ref, kna_ref, vna_ref, qm_ref, km_ref, vm_ref):
    del pos_ref
    h = _rms(x_ref[...], gmix_ref[...]).astype(BF16)

    def proj(lo, hi):
        return jnp.dot(h, win_ref[:, lo:hi], preferred_element_type=F32)

    gsum = gsum_ref[...]

    def head_norm(v, gain):
        ss = jnp.dot((v * v).astype(BF16), gsum, preferred_element_type=F32)
        return (v * lax.rsqrt(ss * (1.0 / NA_HEAD_DIM) + EPS)) * gain

    o1, o2, o3 = NA_WIDTH, 2 * NA_WIDTH, 3 * NA_WIDTH
    o4 = o3 + MLA_Q_LORA
    o5 = o4 + MLA_KV_LORA
    cos = cos_ref[...]
    sin = sin_ref[...]

    def rope(u):
        return u * cos + pltpu.roll(u, 64, 1) * sin

    cq = _rms(proj(o3, o4), gqa_ref[...]).astype(BF16)
    q_all = jnp.dot(cq, wuq_ref[...], preferred_element_type=F32)
    gq = gq_ref[...]
    for hd in range(MLA_HEADS):
        qh = q_all[:, hd * MLA_QK_PAD:(hd + 1) * MLA_QK_PAD]
        r = lax.rsqrt(jnp.sum(qh * qh, axis=-1, keepdims=True) * (1.0 / MLA_QK) + EPS)
        qn = (qh * r) * gq
        qm_ref[hd, 0:MLA_NOPE, :] = qn[:, 0:MLA_NOPE].T.astype(BF16)
        qm_ref[hd, MLA_NOPE:MLA_QK_PAD, :] = rope(qn[:, MLA_NOPE:MLA_QK_PAD]).T.astype(BF16)

    ckv = _rms(proj(o4, o5), gkva_ref[...]).astype(BF16)
    kv = jnp.dot(ckv, wukv_ref[...], preferred_element_type=F32)
    kpe = proj(o5, o5 + LANES)
    ss_pe = jnp.sum(kpe * kpe, axis=-1, keepdims=True)
    gk = gk_ref[...]
    for hd in range(MLA_HEADS):
        base = hd * (MLA_NOPE + MLA_V)
        kn = kv[:, base:base + MLA_NOPE]
        r = lax.rsqrt((jnp.sum(kn * kn, axis=-1, keepdims=True) + ss_pe) * (1.0 / MLA_QK) + EPS)
        km_ref[hd, :, 0:MLA_NOPE] = ((kn * r) * gk[:, 0:MLA_NOPE]).astype(BF16)
        km_ref[hd, :, MLA_NOPE:MLA_QK_PAD] = rope((kpe * r) * gk[:, MLA_NOPE:MLA_QK_PAD]).astype(BF16)
        vm_ref[hd, 0] = kv[:, base + MLA_NOPE:base + MLA_NOPE + MLA_V].T.astype(BF16)

    qna_ref[...] = head_norm(proj(0, o1), gqna_ref[...]).astype(BF16)
    kna_ref[...] = head_norm(proj(o1, o2), gkna_ref[...]).astype(BF16)
    vna_ref[...] = proj(o2, o3).astype(BF16)


def _proj(x, pos_blk, lw, cos_t, sin_t):
    n = x.shape[0]
    tm = TOKEN_TILE
    row = lambda i, pos: (i, 0)
    head = lambda i, pos: (0, i, 0)
    in_specs = [
        pl.BlockSpec((tm, D_MODEL), row),
        _const_spec((1, D_MODEL)),
        _const_spec(lw["w_in"].shape),
        _const_spec((NA_WIDTH, NA_WIDTH)),
        _const_spec((1, NA_WIDTH)),
        _const_spec((1, NA_WIDTH)),
        _const_spec((1, MLA_Q_LORA)),
        _const_spec(lw["w_uq"].shape),
        _const_spec((1, MLA_KV_LORA)),
        _const_spec(lw["w_ukv"].shape),
        _const_spec((1, MLA_QK_PAD)),
        _const_spec((1, MLA_QK_PAD)),
        pl.BlockSpec((tm, LANES), lambda i, pos: (pos[i], 0)),
        pl.BlockSpec((tm, LANES), lambda i, pos: (pos[i], 0)),
    ]
    out_specs = [
        pl.BlockSpec((tm, NA_WIDTH), row),
        pl.BlockSpec((tm, NA_WIDTH), row),
        pl.BlockSpec((tm, NA_WIDTH), row),
        pl.BlockSpec((MLA_HEADS, MLA_QK_PAD, tm), lambda i, pos: (0, 0, i)),
        pl.BlockSpec((MLA_HEADS, tm, MLA_QK_PAD), head),
        pl.BlockSpec((MLA_HEADS, 1, MLA_V, tm), lambda i, pos: (0, i, 0, 0)),
    ]
    out_shape = [
        jax.ShapeDtypeStruct((n, NA_WIDTH), BF16),
        jax.ShapeDtypeStruct((n, NA_WIDTH), BF16),
        jax.ShapeDtypeStruct((n, NA_WIDTH), BF16),
        jax.ShapeDtypeStruct((MLA_HEADS, MLA_QK_PAD, n), BF16),
        jax.ShapeDtypeStruct((MLA_HEADS, n, MLA_QK_PAD), BF16),
        jax.ShapeDtypeStruct((MLA_HEADS, n // tm, MLA_V, tm), BF16),
    ]
    return pl.pallas_call(
        _proj_kernel,
        grid_spec=pltpu.PrefetchScalarGridSpec(
            num_scalar_prefetch=1, grid=(n // tm,), in_specs=in_specs, out_specs=out_specs),
        out_shape=out_shape,
        compiler_params=_cparams(("arbitrary",)),
        name="proj",
    )(pos_blk, x, lw["g_mix"], lw["w_in"], lw["gsum"], lw["g_qna"], lw["g_kna"], lw["g_qa"], lw["w_uq"],
      lw["g_kva"], lw["w_ukv"], lw["g_q"], lw["g_k"], cos_t, sin_t)


def _na_kernel(w0_ref, kind_ref, q_ref, k_ref, v_ref, bias_ref, o_ref):
    del w0_ref
    kind = kind_ref[pl.program_id(0)]
    is_first = kind == 0
    is_last = kind == 2
    lane = lax.broadcasted_iota(jnp.int32, (GRID_W, LANES), 1)
    lo_half = lane < NA_HEAD_DIM
    half_rows = NA_ROWS // 2

    def one_row(j, carry):
        sh = j - half_rows
        off = jnp.where(is_first, jnp.maximum(sh, 0), jnp.where(is_last, NA_BLOCK_ROWS + jnp.minimum(sh, 0), j))
        dcls = jnp.where(is_first, jnp.minimum(j, half_rows), jnp.where(is_last, jnp.maximum(j, half_rows), half_rows))
        kstart = pl.multiple_of(off * GRID_W, GRID_W)
        qstart = pl.multiple_of(j * GRID_W, GRID_W)
        scores = []
        for p in range(NA_HEADS // 2):
            cols = slice(p * LANES, (p + 1) * LANES)
            qp = q_ref[pl.ds(qstart, GRID_W), cols]
            kp = k_ref[pl.ds(kstart, NA_ROWS * GRID_W), cols]
            zero = jnp.zeros_like(qp)
            q2 = jnp.concatenate([jnp.where(lo_half, qp, zero), jnp.where(lo_half, zero, qp)], axis=0)
            s = lax.dot_general(q2, kp, (((1,), (1,)), ((), ())), preferred_element_type=F32)
            scores.append(s + bias_ref[dcls, p])
        s = jnp.concatenate(scores, axis=0)
        e = jnp.exp2(s - jnp.max(s, axis=-1, keepdims=True))
        inv_l = 1.0 / jnp.sum(e, axis=-1, keepdims=True)
        pb = e.astype(BF16)
        for p in range(NA_HEADS // 2):
            cols = slice(p * LANES, (p + 1) * LANES)
            rows = slice(2 * p * GRID_W, (2 * p + 2) * GRID_W)
            vp = v_ref[pl.ds(kstart, NA_ROWS * GRID_W), cols]
            o2 = jnp.dot(pb[rows], vp, preferred_element_type=F32) * inv_l[rows]
            o_ref[pl.ds(qstart, GRID_W), cols] = jnp.where(lo_half, o2[0:GRID_W], o2[GRID_W:2 * GRID_W])
        return carry

    lax.fori_loop(0, NA_BLOCK_ROWS, one_row, 0, unroll=NA_ROW_UNROLL)


def _na_bias_table(rpb):
    c = np.arange(GRID_W)
    cs = np.clip(c - NA_COLS // 2, 0, GRID_W - NA_COLS)
    kc = np.arange(GRID_W)
    valid = (kc[None, :] >= cs[:, None]) & (kc[None, :] < cs[:, None] + NA_COLS)
    dc = kc[None, :] - c[:, None] + (NA_COLS - 1)
    onehot = ((np.arange(2 * NA_COLS - 1)[:, None, None] == dc[None]) & valid[None]).astype(np.float32)
    t = jnp.einsum("hrd,dck->hrck", rpb.astype(F32), jnp.asarray(onehot), precision=lax.Precision.HIGHEST)
    t = jnp.where(jnp.asarray(valid)[None, None], t * LOG2E, NEG_INF)
    tab = jnp.stack([t[:, NA_ROWS - 1 - d:2 * NA_ROWS - 1 - d] for d in range(NA_ROWS)], axis=0)
    tab = jnp.transpose(tab, (0, 1, 3, 2, 4))
    return tab.reshape(NA_ROWS, NA_HEADS // 2, 2 * GRID_W, NA_ROWS * GRID_W)


def _na_block_tables(segments):
    w0, kind = [], []
    tok = 0
    for b, s in segments:
        rows = s // GRID_W
        nb = rows // NA_BLOCK_ROWS
        for _ in range(b):
            for i in range(nb):
                r0 = min(max(NA_BLOCK_ROWS * i - NA_ROWS // 2, 0), rows - NA_WINDOW_ROWS)
                w0.append(tok // GRID_W + r0)
                kind.append(0 if i == 0 else (2 if i == nb - 1 else 1))
            tok += s
    return jnp.asarray(w0, jnp.int32), jnp.asarray(kind, jnp.int32)


def _na(q, k, v, bias, segments):
    n = q.shape[0]
    tq = NA_BLOCK_ROWS * GRID_W
    tw = NA_WINDOW_ROWS * GRID_W
    w0, kind = _na_block_tables(segments)
    win = pl.BlockSpec((pl.Element(tw), pl.Element(NA_WIDTH)), lambda i, w0, kind: (w0[i] * GRID_W, 0))
    return pl.pallas_call(
        _na_kernel,
        grid_spec=pltpu.PrefetchScalarGridSpec(
            num_scalar_prefetch=2,
            grid=(n // tq,),
            in_specs=[
                pl.BlockSpec((tq, NA_WIDTH), lambda i, w0, kind: (i, 0)),
                win,
                win,
                pl.BlockSpec(bias.shape, lambda i, w0, kind: (0, 0, 0, 0), pipeline_mode=pl.Buffered(1)),
            ],
            out_specs=pl.BlockSpec((tq, NA_WIDTH), lambda i, w0, kind: (i, 0)),
        ),
        out_shape=jax.ShapeDtypeStruct((n, NA_WIDTH), F32),
        compiler_params=_cparams(("arbitrary",)),
        name="na",
    )(w0, kind, q, k, v, bias)


def _flash_kernel(prev_ref, qt_ref, k_ref, vt_ref, o_ref, s_scr, acc_ref, *, n_chunks, tk, per_trip):
    del prev_ref
    acc_ref[...] = jnp.zeros(acc_ref.shape, F32)
    qt = qt_ref[...]
    tq = qt.shape[1]

    def scores(c, slot):
        start = pl.multiple_of(c * tk, tk)
        s_scr[slot] = jnp.dot(k_ref[pl.ds(start, tk), :], qt, preferred_element_type=F32)

    def softmax_pv(c, slot, m_prev, l_prev):
        st = s_scr[slot]
        m_new = jnp.maximum(m_prev, jnp.max(st, axis=0, keepdims=True))
        alpha = jnp.exp2(m_prev - m_new)
        pt = jnp.exp2(st - m_new)
        l_new = alpha * l_prev + jnp.sum(pt, axis=0, keepdims=True)
        acc_ref[...] = alpha * acc_ref[...] + jnp.dot(vt_ref[c], pt.astype(BF16), preferred_element_type=F32)
        return m_new, l_new

    scores(0, 0)

    def body(i, carry):
        m, l = carry
        for u in range(per_trip // 2):
            c0 = per_trip * i + 2 * u
            scores(c0 + 1, 1)
            m, l = softmax_pv(c0, 0, m, l)
            scores(jnp.minimum(c0 + 2, n_chunks - 1), 0)
            m, l = softmax_pv(c0 + 1, 1, m, l)
        return m, l

    init = (jnp.full((1, tq), -jnp.inf, F32), jnp.zeros((1, tq), F32))
    _, l = lax.fori_loop(0, n_chunks // per_trip, body, init)
    o_ref[...] = (acc_ref[...] / l).T


def _flash(prev, qt, k, vt, tok_off, b, s):
    n = k.shape[1]
    tq, tk = FLASH_TQ, FLASH_TK
    assert tok_off % s == 0 and s % tq == 0 and s % (2 * tk) == 0 and vt.shape[-1] == tk
    nq = s // tq
    nc = s // tk
    per_trip = math.gcd(nc, FLASH_CHUNKS_PER_TRIP)
    qblk0 = tok_off // tq
    sblk0 = tok_off // s
    return pl.pallas_call(
        functools.partial(_flash_kernel, n_chunks=nc, tk=tk, per_trip=per_trip),
        grid=(b, MLA_HEADS, nq),
        in_specs=[
            pl.BlockSpec(memory_space=pl.ANY),
            pl.BlockSpec((None, MLA_QK_PAD, tq), lambda bi, h, i: (h, 0, qblk0 + bi * nq + i)),
            pl.BlockSpec((None, s, MLA_QK_PAD), lambda bi, h, i: (h, sblk0 + bi, 0)),
            pl.BlockSpec((None, nc, MLA_V, tk), lambda bi, h, i: (h, sblk0 + bi, 0, 0)),
        ],
        out_specs=pl.BlockSpec((tq, MLA_V), lambda bi, h, i: (qblk0 + bi * nq + i, h)),
        out_shape=jax.ShapeDtypeStruct((n, MLA_WIDTH), F32),
        scratch_shapes=[pltpu.VMEM((2, tk, tq), F32), pltpu.VMEM((MLA_V, tq), F32)],
        input_output_aliases={0: 0},
        compiler_params=_cparams(("arbitrary", "arbitrary", "arbitrary")),
        name="flash",
    )(prev, qt, k, vt)


def _merge_kernel(*refs, route):
    if route:
        (x_ref, a_ref, b_ref, ga_ref, gb_ref, wout_ref, gffn_ref, wr_ref,
         xo_ref, h_ref, idx_ref, gate_ref) = refs
    else:
        x_ref, a_ref, b_ref, ga_ref, gb_ref, wout_ref, gffn_ref, xo_ref, h_ref = refs
    a = _rms(a_ref[...], ga_ref[...]).astype(BF16)
    b = _rms(b_ref[...], gb_ref[...]).astype(BF16)
    y = (jnp.dot(a, wout_ref[0:NA_WIDTH, :], preferred_element_type=F32)
         + jnp.dot(b, wout_ref[NA_WIDTH:NA_WIDTH + MLA_WIDTH, :], preferred_element_type=F32))
    xn = x_ref[...] + y
    xo_ref[...] = xn
    hf = _rms(xn, gffn_ref[...])
    h_ref[...] = hf.astype(h_ref.dtype)
    if route:
        hh, hl = _split_bf16(hf)
        r1 = jnp.dot(hh, wr_ref[...], preferred_element_type=F32)
        r2 = jnp.dot(hl, wr_ref[:, 0:LANES], preferred_element_type=F32)
        logits = r1[:, 0:LANES] + (r1[:, LANES:2 * LANES] + r2)
        lg = logits.T[0:N_EXPERTS, :]
        eid = lax.broadcasted_iota(jnp.int32, lg.shape, 0)
        t1 = jnp.max(lg, axis=0, keepdims=True)
        i1 = jnp.min(jnp.where(lg == t1, eid, N_EXPERTS), axis=0, keepdims=True)
        lg2 = jnp.where(eid == i1, -jnp.inf, lg)
        t2 = jnp.max(lg2, axis=0, keepdims=True)
        i2 = jnp.min(jnp.where(lg2 == t2, eid, N_EXPERTS), axis=0, keepdims=True)
        e2 = jnp.exp(t2 - t1)
        den = 1.0 + e2
        idx_ref[...] = jnp.where(eid == 0, i1, jnp.where(eid == 1, i2, 0))
        gate_ref[...] = jnp.where(eid == 0, 1.0 / den, jnp.where(eid == 1, e2 / den, 0.0))


def _merge(x, out_a, out_b, lw, route):
    n = x.shape[0]
    tm = TOKEN_TILE
    row = lambda i: (i, 0)
    in_specs = [
        pl.BlockSpec((tm, D_MODEL), row),
        pl.BlockSpec((tm, NA_WIDTH), row),
        pl.BlockSpec((tm, MLA_WIDTH), row),
        _const_spec((1, NA_WIDTH)),
        _const_spec((1, MLA_WIDTH)),
        _const_spec((NA_WIDTH + MLA_WIDTH, D_MODEL)),
        _const_spec((1, D_MODEL)),
    ]
    args = [x, out_a, out_b, lw["g_oa"], lw["g_ob"], lw["w_out"], lw["g_ffn"]]
    out_specs = [pl.BlockSpec((tm, D_MODEL), row), pl.BlockSpec((tm, D_MODEL), row)]
    out_shape = [jax.ShapeDtypeStruct((n, D_MODEL), F32),
                 jax.ShapeDtypeStruct((n, D_MODEL), F32 if route else BF16)]
    if route:
        in_specs += [_const_spec((D_MODEL, 2 * LANES))]
        args += [lw["w_router"]]
        col = lambda i: (0, i)
        out_specs += [pl.BlockSpec((N_EXPERTS, tm), col), pl.BlockSpec((N_EXPERTS, tm), col)]
        out_shape += [jax.ShapeDtypeStruct((N_EXPERTS, n), jnp.int32), jax.ShapeDtypeStruct((N_EXPERTS, n), F32)]
    return pl.pallas_call(
        functools.partial(_merge_kernel, route=route),
        grid=(n // tm,),
        in_specs=in_specs,
        out_specs=out_specs,
        out_shape=out_shape,
        compiler_params=_cparams(("arbitrary",)),
        name="merge_route" if route else "merge",
    )(*args)


def _ffn_kernel(*refs, residual):
    if residual:
        te_ref, src_ref, nv_ref, h_ref, x_ref, wg_ref, wu_ref, wd_ref, o_ref = refs
    else:
        te_ref, src_ref, nv_ref, h_ref, wg_ref, wu_ref, wd_ref, o_ref = refs
    del te_ref, src_ref
    d_ff = wg_ref.shape[-1]

    @pl.when(pl.program_id(0) < nv_ref[0])
    def _():
        h = h_ref[...].astype(BF16)
        acc = x_ref[...] if residual else None
        for c in range(d_ff // FFN_CHUNK):
            sl = slice(c * FFN_CHUNK, (c + 1) * FFN_CHUNK)
            g = jnp.dot(h, wg_ref[:, sl], preferred_element_type=F32)
            u = jnp.dot(h, wu_ref[:, sl], preferred_element_type=F32)
            act = ((g * jax.nn.sigmoid(g)) * u).astype(BF16)
            y = jnp.dot(act, wd_ref[sl, :], preferred_element_type=F32)
            acc = y if acc is None else acc + y
        o_ref[...] = acc


def _ffn(h, x, wg, wu, wd, tile_expert, tile_src, n_valid):
    n = h.shape[0]
    tm = TOKEN_TILE
    d_ff = wg.shape[-1]
    assert d_ff % FFN_CHUNK == 0
    row = lambda i, te, src, nv: (src[i], 0)
    wspec = lambda shp: pl.BlockSpec((None,) + shp, lambda i, te, src, nv: (te[i], 0, 0),
                                     pipeline_mode=pl.Buffered(1))
    residual = x is not None
    in_specs = [pl.BlockSpec((tm, D_MODEL), row)]
    args = [h]
    if residual:
        in_specs.append(pl.BlockSpec((tm, D_MODEL), row))
        args.append(x)
    in_specs += [wspec((D_MODEL, d_ff)), wspec((D_MODEL, d_ff)), wspec((d_ff, D_MODEL))]
    args += [wg, wu, wd]
    return pl.pallas_call(
        functools.partial(_ffn_kernel, residual=residual),
        grid_spec=pltpu.PrefetchScalarGridSpec(
            num_scalar_prefetch=3, grid=(n // tm,), in_specs=in_specs,
            out_specs=pl.BlockSpec((tm, D_MODEL), row)),
        out_shape=jax.ShapeDtypeStruct((n, D_MODEL), F32),
        compiler_params=_cparams(("arbitrary",)),
        name="ffn" if residual else "expert_ffn",
    )(tile_expert, tile_src, n_valid, *args)


def _row_copy(src_ref, src_row, dst_ref, dst_row, sem):
    return pltpu.make_async_copy(src_ref.at[pl.ds(src_row, 1), :], dst_ref.at[pl.ds(dst_row, 1), :], sem)


def _dispatch_kernel(pos_ref, h_ref, zero_ref, xs_ref, sem):
    del zero_ref
    tm = h_ref.shape[0]

    def issue(r, c):
        for kk in range(TOP_K):
            _row_copy(h_ref, r, xs_ref, pos_ref[TOP_K * r + kk], sem).start()
        return c

    lax.fori_loop(0, tm, issue, 0, unroll=ROW_DMA_UNROLL)
    for _ in range(TOP_K):
        pltpu.make_async_copy(h_ref, xs_ref.at[pl.ds(0, tm), :], sem).wait()


def _dispatch(h, pos_flat, n_slots):
    n = h.shape[0]
    tm = TOKEN_TILE
    zeros = jnp.zeros((n_slots, D_MODEL), F32)
    return pl.pallas_call(
        _dispatch_kernel,
        grid=(n // tm,),
        in_specs=[
            pl.BlockSpec((TOP_K * tm,), lambda i: (i,), memory_space=pltpu.SMEM),
            pl.BlockSpec((tm, D_MODEL), lambda i: (i, 0)),
            pl.BlockSpec(memory_space=pl.ANY),
        ],
        out_specs=pl.BlockSpec(memory_space=pl.ANY),
        out_shape=jax.ShapeDtypeStruct((n_slots, D_MODEL), F32),
        scratch_shapes=[pltpu.SemaphoreType.DMA(())],
        input_output_aliases={2: 0},
        compiler_params=_cparams(("arbitrary",)),
        name="dispatch",
    )(pos_flat, h, zeros)


def _combine_kernel(pos_ref, x_ref, gate_ref, ys_ref, o_ref, buf_ref, sem):
    tm = x_ref.shape[0]

    def issue(r, c):
        for kk in range(TOP_K):
            _row_copy(ys_ref, pos_ref[TOP_K * r + kk], buf_ref.at[kk], r, sem).start()
        return c

    lax.fori_loop(0, tm, issue, 0, unroll=ROW_DMA_UNROLL)
    for kk in range(TOP_K):
        pltpu.make_async_copy(ys_ref.at[pl.ds(0, tm), :], buf_ref.at[kk], sem).wait()
    gate = gate_ref[...].T
    out = gate[:, 0:1] * buf_ref[0] + gate[:, 1:2] * buf_ref[1]
    o_ref[...] = x_ref[...] + out


def _combine(x, gate_t, ys, pos_flat):
    n = x.shape[0]
    tm = TOKEN_TILE
    return pl.pallas_call(
        _combine_kernel,
        grid=(n // tm,),
        in_specs=[
            pl.BlockSpec((TOP_K * tm,), lambda i: (i,), memory_space=pltpu.SMEM),
            pl.BlockSpec((tm, D_MODEL), lambda i: (i, 0)),
            pl.BlockSpec((N_EXPERTS, tm), lambda i: (0, i)),
            pl.BlockSpec(memory_space=pl.ANY),
        ],
        out_specs=pl.BlockSpec((tm, D_MODEL), lambda i: (i, 0)),
        out_shape=jax.ShapeDtypeStruct((n, D_MODEL), F32),
        scratch_shapes=[pltpu.VMEM((TOP_K, tm, D_MODEL), F32), pltpu.SemaphoreType.DMA(())],
        compiler_params=_cparams(("arbitrary",)),
        name="combine",
    )(pos_flat, x, gate_t, ys)


def _moe(x, h, idx, gate, wg, wu, wd):
    n = x.shape[0]
    tm = TOKEN_TILE
    n_tiles = TOP_K * n // tm + N_EXPERTS
    n_slots = n_tiles * tm
    first = idx[0] < idx[1]
    e_lo = jnp.where(first, idx[0], idx[1])
    e_hi = jnp.where(first, idx[1], idx[0])
    g_lo = jnp.where(first, gate[0], gate[1])
    g_hi = jnp.where(first, gate[1], gate[0])
    experts = jnp.arange(N_EXPERTS, dtype=jnp.int32)
    sel = (e_lo[None, :] == experts[:, None]) | (e_hi[None, :] == experts[:, None])
    sel_b = sel.reshape(N_EXPERTS, n // tm, tm).astype(BF16)
    tri = (jnp.arange(tm)[:, None] <= jnp.arange(tm)[None, :]).astype(BF16)
    within = jnp.einsum("ebj,jt->ebt", sel_b, tri, preferred_element_type=F32).astype(jnp.int32)
    block_tot = within[:, :, -1]
    block_off = jnp.cumsum(block_tot, axis=1) - block_tot
    csum = (within + block_off[:, :, None]).reshape(N_EXPERTS, n)
    counts = csum[:, -1]
    padded = ((counts + tm - 1) // tm) * tm
    ends = jnp.cumsum(padded)
    starts = ends - padded
    slot = starts[:, None] + csum - 1
    pos_lo = jnp.sum(jnp.where(e_lo[None, :] == experts[:, None], slot, 0), axis=0)
    pos_hi = jnp.sum(jnp.where(e_hi[None, :] == experts[:, None], slot, 0), axis=0)
    pos_flat = jnp.stack([pos_lo, pos_hi], axis=1).reshape(-1).astype(jnp.int32)
    n_valid = (ends[-1] // tm).astype(jnp.int32)
    tile_src = jnp.minimum(jnp.arange(n_tiles, dtype=jnp.int32), n_valid - 1)
    tile_expert = jnp.minimum(
        jnp.sum((tile_src[:, None] * tm >= ends[None, :]).astype(jnp.int32), axis=1), N_EXPERTS - 1).astype(jnp.int32)
    gate_t = jnp.concatenate([g_lo[None], g_hi[None], jnp.zeros((N_EXPERTS - TOP_K, n), F32)], axis=0)
    xs = _dispatch(h, pos_flat, n_slots)
    ys = _ffn(xs, None, wg, wu, wd, tile_expert, tile_src, n_valid.reshape(1))
    return _combine(x, gate_t, ys, pos_flat)


def _rope_layout(w64):
    z = jnp.zeros(w64.shape[:-1] + (MLA_ROPE // 2,), w64.dtype)
    return jnp.concatenate([w64[..., :MLA_ROPE // 2], z, w64[..., MLA_ROPE // 2:], z], axis=-1)


def _qk_layout(w192):
    return jnp.concatenate([w192[..., :MLA_NOPE], _rope_layout(w192[..., MLA_NOPE:])], axis=-1)


def _layer_weights(l, norm_mix, norm_ffn, w_in, na_q_norm, na_k_norm, mla_q_a_norm, w_uq, mla_kv_a_norm, w_ukv,
                   mla_q_norm, mla_k_norm, na_out_norm, mla_out_norm, w_out):
    o5 = 3 * NA_WIDTH + MLA_Q_LORA + MLA_KV_LORA
    w_in_l = w_in[l]
    w_in_p = jnp.concatenate([w_in_l[:, :o5], _rope_layout(w_in_l[:, o5:])], axis=1)
    w_uq_p = _qk_layout(w_uq[l].reshape(MLA_Q_LORA, MLA_HEADS, MLA_QK)).reshape(MLA_Q_LORA, MLA_HEADS * MLA_QK_PAD)
    grp = jnp.arange(NA_WIDTH) // NA_HEAD_DIM
    row = lambda v: v.reshape(1, -1).astype(F32)
    return {
        "g_mix": row(norm_mix[l]),
        "g_ffn": row(norm_ffn[l]),
        "w_in": w_in_p.astype(BF16),
        "gsum": (grp[:, None] == grp[None, :]).astype(BF16),
        "g_qna": row(jnp.tile(na_q_norm[l], NA_HEADS) * (NA_HEAD_DIM ** -0.5 * LOG2E)),
        "g_kna": row(jnp.tile(na_k_norm[l], NA_HEADS)),
        "g_qa": row(mla_q_a_norm[l]),
        "w_uq": w_uq_p.astype(BF16),
        "g_kva": row(mla_kv_a_norm[l]),
        "w_ukv": w_ukv[l].astype(BF16),
        "g_q": row(_qk_layout(mla_q_norm[l]) * (MLA_QK ** -0.5 * LOG2E)),
        "g_k": row(_qk_layout(mla_k_norm[l])),
        "g_oa": row(na_out_norm[l]),
        "g_ob": row(mla_out_norm[l]),
        "w_out": w_out[l].astype(BF16),
    }


def _rope_tables(s_max):
    inv = ROPE_THETA ** (-jnp.arange(0, MLA_ROPE, 2, dtype=F32) / MLA_ROPE)
    ang = jnp.arange(s_max, dtype=F32)[:, None] * inv[None, :]
    cos, sin = jnp.cos(ang), jnp.sin(ang)
    z = jnp.zeros_like(cos)
    return jnp.concatenate([cos, z, cos, z], axis=1), jnp.concatenate([-sin, z, sin, z], axis=1)


def _trunk(x, segments, norm_mix, norm_ffn, w_in, na_q_norm, na_k_norm, na_rpb, mla_q_a_norm, w_uq, mla_kv_a_norm,
           w_ukv, mla_q_norm, mla_k_norm, na_out_norm, mla_out_norm, w_out, ffn_w_gate, ffn_w_up, ffn_w_down,
           moe_router, moe_w_gate, moe_w_up, moe_w_down):
    n = x.shape[0]
    tm = TOKEN_TILE
    depth = w_in.shape[0]
    s_max = max(s for _, s in segments)
    cos_t, sin_t = _rope_tables(s_max)
    pos_blk = jnp.asarray([i for b, s in segments for _ in range(b) for i in range(s // tm)], jnp.int32)
    n_tiles = n // tm
    dense_tiles = (jnp.zeros((n_tiles,), jnp.int32), jnp.arange(n_tiles, dtype=jnp.int32),
                   jnp.full((1,), n_tiles, jnp.int32))
    for l in range(depth):
        lw = _layer_weights(l, norm_mix, norm_ffn, w_in, na_q_norm, na_k_norm, mla_q_a_norm, w_uq, mla_kv_a_norm,
                            w_ukv, mla_q_norm, mla_k_norm, na_out_norm, mla_out_norm, w_out)
        qna, kna, vna, qm, km, vm = _proj(x, pos_blk, lw, cos_t, sin_t)
        out_a = _na(qna, kna, vna, _na_bias_table(na_rpb[l]), segments)
        out_b = jnp.zeros((n, MLA_WIDTH), F32)
        tok = 0
        for b, s in segments:
            out_b = _flash(out_b, qm, km, vm, tok, b, s)
            tok += b * s
        i = l // 2
        if l % 2 == 0:
            x, h = _merge(x, out_a, out_b, lw, route=False)
            x = _ffn(h, x, ffn_w_gate[i:i + 1].astype(BF16), ffn_w_up[i:i + 1].astype(BF16),
                     ffn_w_down[i:i + 1].astype(BF16), *dense_tiles)
        else:
            wr = jnp.pad(moe_router[i], ((0, 0), (0, LANES - N_EXPERTS)))
            lw["w_router"] = jnp.concatenate(_split_bf16(wr), axis=1)
            x, h, idx, gate = _merge(x, out_a, out_b, lw, route=True)
            x = _moe(x, h, idx[:TOP_K], gate[:TOP_K], moe_w_gate[i].astype(BF16), moe_w_up[i].astype(BF16),
                     moe_w_down[i].astype(BF16))
    return x


def kernel(x_prompt, x_sample, norm_mix, norm_ffn, w_in, na_q_norm, na_k_norm, na_rpb, mla_q_a_norm, w_uq,
           mla_kv_a_norm, w_ukv, mla_q_norm, mla_k_norm, na_out_norm, mla_out_norm, w_out, ffn_w_gate, ffn_w_up,
           ffn_w_down, moe_router, moe_w_gate, moe_w_up, moe_w_down):
    bp, sp, d = x_prompt.shape
    bs, ss, _ = x_sample.shape
    segments = ((bp, sp), (bs, ss))
    x = jnp.concatenate([x_prompt.reshape(bp * sp, d), x_sample.reshape(bs * ss, d)], axis=0)
    y = _trunk(x, segments, norm_mix, norm_ffn, w_in, na_q_norm, na_k_norm, na_rpb, mla_q_a_norm, w_uq,
               mla_kv_a_norm, w_ukv, mla_q_norm, mla_k_norm, na_out_norm, mla_out_norm, w_out, ffn_w_gate,
               ffn_w_up, ffn_w_down, moe_router, moe_w_gate, moe_w_up, moe_w_down)
    n_p = bp * sp
    return y[:n_p].reshape(bp, sp, d), y[n_p:].reshape(bs, ss, d)
```

```python
import functools
import math

import jax
import jax.numpy as jnp
import numpy as np
from jax import lax
from jax.experimental import pallas as pl
from jax.experimental.pallas import tpu as pltpu

F32 = jnp.float32
BF16 = jnp.bfloat16

D_MODEL = 1024
GRID_W = 64
NA_HEADS = 8
NA_HEAD_DIM = 64
NA_ROWS = 8
NA_COLS = 16
NA_WIDTH = NA_HEADS * NA_HEAD_DIM
MLA_HEADS = 4
MLA_Q_LORA = 384
MLA_KV_LORA = 256
MLA_NOPE = 128
MLA_ROPE = 64
MLA_V = 128
MLA_QK = MLA_NOPE + MLA_ROPE
MLA_WIDTH = MLA_HEADS * MLA_V
MLA_QK_PAD = 256
ROPE_THETA = 10000.0
N_EXPERTS = 8
TOP_K = 2
EPS = 1e-6
NEG_INF = -1e30
LOG2E = math.log2(math.e)

LANES = 128
TOKEN_TILE = 512
NA_BLOCK_ROWS = 8
NA_WINDOW_ROWS = 16
NA_ROW_UNROLL = 4
FLASH_TQ = 1024
FLASH_TK = TOKEN_TILE
FLASH_CHUNKS_PER_TRIP = 8
ROW_DMA_UNROLL = 8
FFN_CHUNK = 1408
VMEM_LIMIT = 56 * 1024 * 1024


def _cparams(sem):
    return pltpu.CompilerParams(dimension_semantics=sem, vmem_limit_bytes=VMEM_LIMIT)


def _const_spec(shape):
    nd = len(shape)
    return pl.BlockSpec(shape, lambda *_: (0,) * nd, pipeline_mode=pl.Buffered(1))


def _rms(xf, g):
    y = xf * lax.rsqrt(jnp.mean(xf * xf, axis=-1, keepdims=True) + EPS)
    return y * g


def _split_bf16(v):
    hi = v.astype(BF16)
    lo = (v - hi.astype(F32)).astype(BF16)
    return hi, lo


def _part_starts(parts, tm):
    starts = [0]
    for a in parts:
        assert a.shape[0] % tm == 0
        starts.append(starts[-1] + a.shape[0] // tm)
    return tuple(starts)


def _part_index(i, *_, lo, nt):
    return (jnp.clip(i - lo, 0, nt - 1), 0)


def _part_specs(parts, tm, width):
    starts = _part_starts(parts, tm)
    return [pl.BlockSpec((tm, width), functools.partial(_part_index, lo=lo, nt=hi - lo))
            for lo, hi in zip(starts[:-1], starts[1:])]


def _read_parts(i, refs, starts):
    v = refs[0][...]
    for r, lo in zip(refs[1:], starts[1:]):
        v = jnp.where(i >= lo, r[...], v)
    return v


def _write_parts(i, refs, starts, v):
    if len(refs) == 1:
        refs[0][...] = v
        return
    for r, lo, hi in zip(refs, starts[:-1], starts[1:]):
        @pl.when(jnp.logical_and(i >= lo, i < hi))
        def _(r=r):
            r[...] = v


def _proj_kernel(pos_ref, *refs, x_starts):
    del pos_ref
    n_x = len(x_starts) - 1
    x_refs = refs[:n_x]
    (gmix_ref, win_ref, gsum_ref, gqna_ref, gkna_ref, gqa_ref, wuq_ref, gkva_ref, wukv_ref, gq_ref, gk_ref,
     cos_ref, sin_ref, qna_ref, kna_ref, vna_ref, qm_ref, km_ref, vm_ref) = refs[n_x:]
    h = _rms(_read_parts(pl.program_id(0), x_refs, x_starts), gmix_ref[...]).astype(BF16)

    def proj(lo, hi):
        return jnp.dot(h, win_ref[:, lo:hi], preferred_element_type=F32)

    gsum = gsum_ref[...]

    def head_norm(v, gain):
        ss = jnp.dot((v * v).astype(BF16), gsum, preferred_element_type=F32)
        return (v * lax.rsqrt(ss * (1.0 / NA_HEAD_DIM) + EPS)) * gain

    o1, o2, o3 = NA_WIDTH, 2 * NA_WIDTH, 3 * NA_WIDTH
    o4 = o3 + MLA_Q_LORA
    o5 = o4 + MLA_KV_LORA
    cos = cos_ref[...]
    sin = sin_ref[...]

    def rope(u):
        return u * cos + pltpu.roll(u, 64, 1) * sin

    cq = _rms(proj(o3, o4), gqa_ref[...]).astype(BF16)
    q_all = jnp.dot(cq, wuq_ref[...], preferred_element_type=F32)
    gq = gq_ref[...]
    for hd in range(MLA_HEADS):
        qh = q_all[:, hd * MLA_QK_PAD:(hd + 1) * MLA_QK_PAD]
        r = lax.rsqrt(jnp.sum(qh * qh, axis=-1, keepdims=True) * (1.0 / MLA_QK) + EPS)
        qn = (qh * r) * gq
        qm_ref[hd, 0:MLA_NOPE, :] = qn[:, 0:MLA_NOPE].T.astype(BF16)
        qm_ref[hd, MLA_NOPE:MLA_QK_PAD, :] = rope(qn[:, MLA_NOPE:MLA_QK_PAD]).T.astype(BF16)

    ckv = _rms(proj(o4, o5), gkva_ref[...]).astype(BF16)
    kv = jnp.dot(ckv, wukv_ref[...], preferred_element_type=F32)
    kpe = proj(o5, o5 + LANES)
    ss_pe = jnp.sum(kpe * kpe, axis=-1, keepdims=True)
    gk = gk_ref[...]
    for hd in range(MLA_HEADS):
        base = hd * (MLA_NOPE + MLA_V)
        kn = kv[:, base:base + MLA_NOPE]
        r = lax.rsqrt((jnp.sum(kn * kn, axis=-1, keepdims=True) + ss_pe) * (1.0 / MLA_QK) + EPS)
        km_ref[hd, :, 0:MLA_NOPE] = ((kn * r) * gk[:, 0:MLA_NOPE]).astype(BF16)
        km_ref[hd, :, MLA_NOPE:MLA_QK_PAD] = rope((kpe * r) * gk[:, MLA_NOPE:MLA_QK_PAD]).astype(BF16)
        vm_ref[hd, 0] = kv[:, base + MLA_NOPE:base + MLA_NOPE + MLA_V].T.astype(BF16)

    qna_ref[...] = head_norm(proj(0, o1), gqna_ref[...]).astype(BF16)
    kna_ref[...] = head_norm(proj(o1, o2), gkna_ref[...]).astype(BF16)
    vna_ref[...] = proj(o2, o3).astype(BF16)


def _proj(x_parts, pos_blk, lw, cos_t, sin_t):
    tm = TOKEN_TILE
    x_starts = _part_starts(x_parts, tm)
    n = x_starts[-1] * tm
    row = lambda i, pos: (i, 0)
    head = lambda i, pos: (0, i, 0)
    in_specs = _part_specs(x_parts, tm, D_MODEL) + [
        _const_spec((1, D_MODEL)),
        _const_spec(lw["w_in"].shape),
        _const_spec((NA_WIDTH, NA_WIDTH)),
        _const_spec((1, NA_WIDTH)),
        _const_spec((1, NA_WIDTH)),
        _const_spec((1, MLA_Q_LORA)),
        _const_spec(lw["w_uq"].shape),
        _const_spec((1, MLA_KV_LORA)),
        _const_spec(lw["w_ukv"].shape),
        _const_spec((1, MLA_QK_PAD)),
        _const_spec((1, MLA_QK_PAD)),
        pl.BlockSpec((tm, LANES), lambda i, pos: (pos[i], 0)),
        pl.BlockSpec((tm, LANES), lambda i, pos: (pos[i], 0)),
    ]
    out_specs = [
        pl.BlockSpec((tm, NA_WIDTH), row),
        pl.BlockSpec((tm, NA_WIDTH), row),
        pl.BlockSpec((tm, NA_WIDTH), row),
        pl.BlockSpec((MLA_HEADS, MLA_QK_PAD, tm), lambda i, pos: (0, 0, i)),
        pl.BlockSpec((MLA_HEADS, tm, MLA_QK_PAD), head),
        pl.BlockSpec((MLA_HEADS, 1, MLA_V, tm), lambda i, pos: (0, i, 0, 0)),
    ]
    out_shape = [
        jax.ShapeDtypeStruct((n, NA_WIDTH), BF16),
        jax.ShapeDtypeStruct((n, NA_WIDTH), BF16),
        jax.ShapeDtypeStruct((n, NA_WIDTH), BF16),
        jax.ShapeDtypeStruct((MLA_HEADS, MLA_QK_PAD, n), BF16),
        jax.ShapeDtypeStruct((MLA_HEADS, n, MLA_QK_PAD), BF16),
        jax.ShapeDtypeStruct((MLA_HEADS, n // tm, MLA_V, tm), BF16),
    ]
    return pl.pallas_call(
        functools.partial(_proj_kernel, x_starts=x_starts),
        grid_spec=pltpu.PrefetchScalarGridSpec(
            num_scalar_prefetch=1, grid=(n // tm,), in_specs=in_specs, out_specs=out_specs),
        out_shape=out_shape,
        compiler_params=_cparams(("arbitrary",)),
        name="proj",
    )(pos_blk, *x_parts, lw["g_mix"], lw["w_in"], lw["gsum"], lw["g_qna"], lw["g_kna"], lw["g_qa"], lw["w_uq"],
      lw["g_kva"], lw["w_ukv"], lw["g_q"], lw["g_k"], cos_t, sin_t)


def _na_kernel(w0_ref, kind_ref, q_ref, k_ref, v_ref, bias_ref, o_ref):
    del w0_ref
    kind = kind_ref[pl.program_id(0)]
    is_first = kind == 0
    is_last = kind == 2
    lane = lax.broadcasted_iota(jnp.int32, (GRID_W, LANES), 1)
    lo_half = lane < NA_HEAD_DIM
    half_rows = NA_ROWS // 2

    def one_row(j, carry):
        sh = j - half_rows
        off = jnp.where(is_first, jnp.maximum(sh, 0), jnp.where(is_last, NA_BLOCK_ROWS + jnp.minimum(sh, 0), j))
        dcls = jnp.where(is_first, jnp.minimum(j, half_rows), jnp.where(is_last, jnp.maximum(j, half_rows), half_rows))
        kstart = pl.multiple_of(off * GRID_W, GRID_W)
        qstart = pl.multiple_of(j * GRID_W, GRID_W)
        scores = []
        for p in range(NA_HEADS // 2):
            cols = slice(p * LANES, (p + 1) * LANES)
            qp = q_ref[pl.ds(qstart, GRID_W), cols]
            kp = k_ref[pl.ds(kstart, NA_ROWS * GRID_W), cols]
            zero = jnp.zeros_like(qp)
            q2 = jnp.concatenate([jnp.where(lo_half, qp, zero), jnp.where(lo_half, zero, qp)], axis=0)
            s = lax.dot_general(q2, kp, (((1,), (1,)), ((), ())), preferred_element_type=F32)
            scores.append(s + bias_ref[dcls, p])
        s = jnp.concatenate(scores, axis=0)
        e = jnp.exp2(s - jnp.max(s, axis=-1, keepdims=True))
        inv_l = 1.0 / jnp.sum(e, axis=-1, keepdims=True)
        pb = e.astype(BF16)
        for p in range(NA_HEADS // 2):
            cols = slice(p * LANES, (p + 1) * LANES)
            rows = slice(2 * p * GRID_W, (2 * p + 2) * GRID_W)
            vp = v_ref[pl.ds(kstart, NA_ROWS * GRID_W), cols]
            o2 = jnp.dot(pb[rows], vp, preferred_element_type=F32) * inv_l[rows]
            o_ref[pl.ds(qstart, GRID_W), cols] = jnp.where(lo_half, o2[0:GRID_W], o2[GRID_W:2 * GRID_W])
        return carry

    lax.fori_loop(0, NA_BLOCK_ROWS, one_row, 0, unroll=NA_ROW_UNROLL)


def _na_bias_table(rpb):
    c = np.arange(GRID_W)
    cs = np.clip(c - NA_COLS // 2, 0, GRID_W - NA_COLS)
    kc = np.arange(GRID_W)
    valid = (kc[None, :] >= cs[:, None]) & (kc[None, :] < cs[:, None] + NA_COLS)
    dc = kc[None, :] - c[:, None] + (NA_COLS - 1)
    onehot = ((np.arange(2 * NA_COLS - 1)[:, None, None] == dc[None]) & valid[None]).astype(np.float32)
    t = jnp.einsum("hrd,dck->hrck", rpb.astype(F32), jnp.asarray(onehot), precision=lax.Precision.HIGHEST)
    t = jnp.where(jnp.asarray(valid)[None, None], t * LOG2E, NEG_INF)
    tab = jnp.stack([t[:, NA_ROWS - 1 - d:2 * NA_ROWS - 1 - d] for d in range(NA_ROWS)], axis=0)
    tab = jnp.transpose(tab, (0, 1, 3, 2, 4))
    return tab.reshape(NA_ROWS, NA_HEADS // 2, 2 * GRID_W, NA_ROWS * GRID_W)


def _na_block_tables(segments):
    w0, kind = [], []
    tok = 0
    for b, s in segments:
        rows = s // GRID_W
        nb = rows // NA_BLOCK_ROWS
        for _ in range(b):
            for i in range(nb):
                r0 = min(max(NA_BLOCK_ROWS * i - NA_ROWS // 2, 0), rows - NA_WINDOW_ROWS)
                w0.append(tok // GRID_W + r0)
                kind.append(0 if i == 0 else (2 if i == nb - 1 else 1))
            tok += s
    return jnp.asarray(w0, jnp.int32), jnp.asarray(kind, jnp.int32)


def _na(q, k, v, bias, segments):
    n = q.shape[0]
    tq = NA_BLOCK_ROWS * GRID_W
    tw = NA_WINDOW_ROWS * GRID_W
    w0, kind = _na_block_tables(segments)
    win = pl.BlockSpec((pl.Element(tw), pl.Element(NA_WIDTH)), lambda i, w0, kind: (w0[i] * GRID_W, 0))
    return pl.pallas_call(
        _na_kernel,
        grid_spec=pltpu.PrefetchScalarGridSpec(
            num_scalar_prefetch=2,
            grid=(n // tq,),
            in_specs=[
                pl.BlockSpec((tq, NA_WIDTH), lambda i, w0, kind: (i, 0)),
                win,
                win,
                pl.BlockSpec(bias.shape, lambda i, w0, kind: (0, 0, 0, 0), pipeline_mode=pl.Buffered(1)),
            ],
            out_specs=pl.BlockSpec((tq, NA_WIDTH), lambda i, w0, kind: (i, 0)),
        ),
        out_shape=jax.ShapeDtypeStruct((n, NA_WIDTH), F32),
        compiler_params=_cparams(("arbitrary",)),
        name="na",
    )(w0, kind, q, k, v, bias)


def _flash_kernel(qt_ref, k_ref, vt_ref, o_ref, s_scr, acc_ref, *, n_chunks, tk, per_trip):
    acc_ref[...] = jnp.zeros(acc_ref.shape, F32)
    qt = qt_ref[...]
    tq = qt.shape[1]

    def scores(c, slot):
        start = pl.multiple_of(c * tk, tk)
        s_scr[slot] = jnp.dot(k_ref[pl.ds(start, tk), :], qt, preferred_element_type=F32)

    def softmax_pv(c, slot, m_prev, l_prev):
        st = s_scr[slot]
        m_new = jnp.maximum(m_prev, jnp.max(st, axis=0, keepdims=True))
        alpha = jnp.exp2(m_prev - m_new)
        pt = jnp.exp2(st - m_new)
        l_new = alpha * l_prev + jnp.sum(pt, axis=0, keepdims=True)
        acc_ref[...] = alpha * acc_ref[...] + jnp.dot(vt_ref[c], pt.astype(BF16), preferred_element_type=F32)
        return m_new, l_new

    scores(0, 0)

    def body(i, carry):
        m, l = carry
        for u in range(per_trip // 2):
            c0 = per_trip * i + 2 * u
            scores(c0 + 1, 1)
            m, l = softmax_pv(c0, 0, m, l)
            scores(jnp.minimum(c0 + 2, n_chunks - 1), 0)
            m, l = softmax_pv(c0 + 1, 1, m, l)
        return m, l

    init = (jnp.full((1, tq), -jnp.inf, F32), jnp.zeros((1, tq), F32))
    _, l = lax.fori_loop(0, n_chunks // per_trip, body, init)
    o_ref[...] = (acc_ref[...] / l).T


def _flash(qt, k, vt, tok_off, b, s):
    n = k.shape[1]
    tq, tk = FLASH_TQ, FLASH_TK
    assert tok_off % s == 0 and s % tq == 0 and s % (2 * tk) == 0 and vt.shape[-1] == tk
    nq = s // tq
    nc = s // tk
    per_trip = math.gcd(nc, FLASH_CHUNKS_PER_TRIP)
    qblk0 = tok_off // tq
    sblk0 = tok_off // s
    return pl.pallas_call(
        functools.partial(_flash_kernel, n_chunks=nc, tk=tk, per_trip=per_trip),
        grid=(b, MLA_HEADS, nq),
        in_specs=[
            pl.BlockSpec((None, MLA_QK_PAD, tq), lambda bi, h, i: (h, 0, qblk0 + bi * nq + i)),
            pl.BlockSpec((None, s, MLA_QK_PAD), lambda bi, h, i: (h, sblk0 + bi, 0)),
            pl.BlockSpec((None, nc, MLA_V, tk), lambda bi, h, i: (h, sblk0 + bi, 0, 0)),
        ],
        out_specs=pl.BlockSpec((tq, MLA_V), lambda bi, h, i: (bi * nq + i, h)),
        out_shape=jax.ShapeDtypeStruct((b * s, MLA_WIDTH), F32),
        scratch_shapes=[pltpu.VMEM((2, tk, tq), F32), pltpu.VMEM((MLA_V, tq), F32)],
        compiler_params=_cparams(("arbitrary", "arbitrary", "arbitrary")),
        name="flash",
    )(qt, k, vt)


def _merge_kernel(*refs, route, x_starts, b_starts):
    n_x, n_b = len(x_starts) - 1, len(b_starts) - 1
    x_refs, a_ref, b_refs = refs[:n_x], refs[n_x], refs[n_x + 1:n_x + 1 + n_b]
    rest = refs[n_x + 1 + n_b:]
    if route:
        ga_ref, gb_ref, wout_ref, gffn_ref, wr_ref, xo_ref, h_ref, idx_ref, gate_ref = rest
    else:
        ga_ref, gb_ref, wout_ref, gffn_ref, xo_ref, h_ref = rest
    i = pl.program_id(0)
    a = _rms(a_ref[...], ga_ref[...]).astype(BF16)
    b = _rms(_read_parts(i, b_refs, b_starts), gb_ref[...]).astype(BF16)
    y = (jnp.dot(a, wout_ref[0:NA_WIDTH, :], preferred_element_type=F32)
         + jnp.dot(b, wout_ref[NA_WIDTH:NA_WIDTH + MLA_WIDTH, :], preferred_element_type=F32))
    xn = _read_parts(i, x_refs, x_starts) + y
    xo_ref[...] = xn
    hf = _rms(xn, gffn_ref[...])
    h_ref[...] = hf.astype(h_ref.dtype)
    if route:
        hh, hl = _split_bf16(hf)
        r1 = jnp.dot(hh, wr_ref[...], preferred_element_type=F32)
        r2 = jnp.dot(hl, wr_ref[:, 0:LANES], preferred_element_type=F32)
        logits = r1[:, 0:LANES] + (r1[:, LANES:2 * LANES] + r2)
        lg = logits.T[0:N_EXPERTS, :]
        eid = lax.broadcasted_iota(jnp.int32, lg.shape, 0)
        t1 = jnp.max(lg, axis=0, keepdims=True)
        i1 = jnp.min(jnp.where(lg == t1, eid, N_EXPERTS), axis=0, keepdims=True)
        lg2 = jnp.where(eid == i1, -jnp.inf, lg)
        t2 = jnp.max(lg2, axis=0, keepdims=True)
        i2 = jnp.min(jnp.where(lg2 == t2, eid, N_EXPERTS), axis=0, keepdims=True)
        e2 = jnp.exp(t2 - t1)
        den = 1.0 + e2
        idx_ref[...] = jnp.where(eid == 0, i1, jnp.where(eid == 1, i2, 0))
        gate_ref[...] = jnp.where(eid == 0, 1.0 / den, jnp.where(eid == 1, e2 / den, 0.0))


def _merge(x_parts, out_a, b_parts, lw, route):
    tm = TOKEN_TILE
    x_starts = _part_starts(x_parts, tm)
    b_starts = _part_starts(b_parts, tm)
    n = out_a.shape[0]
    assert x_starts[-1] * tm == n and b_starts[-1] * tm == n
    row = lambda i: (i, 0)
    in_specs = _part_specs(x_parts, tm, D_MODEL) + [pl.BlockSpec((tm, NA_WIDTH), row)] + _part_specs(
        b_parts, tm, MLA_WIDTH) + [
        _const_spec((1, NA_WIDTH)),
        _const_spec((1, MLA_WIDTH)),
        _const_spec((NA_WIDTH + MLA_WIDTH, D_MODEL)),
        _const_spec((1, D_MODEL)),
    ]
    args = [*x_parts, out_a, *b_parts, lw["g_oa"], lw["g_ob"], lw["w_out"], lw["g_ffn"]]
    out_specs = [pl.BlockSpec((tm, D_MODEL), row), pl.BlockSpec((tm, D_MODEL), row)]
    out_shape = [jax.ShapeDtypeStruct((n, D_MODEL), F32),
                 jax.ShapeDtypeStruct((n, D_MODEL), F32 if route else BF16)]
    if route:
        in_specs += [_const_spec((D_MODEL, 2 * LANES))]
        args += [lw["w_router"]]
        col = lambda i: (0, i)
        out_specs += [pl.BlockSpec((N_EXPERTS, tm), col), pl.BlockSpec((N_EXPERTS, tm), col)]
        out_shape += [jax.ShapeDtypeStruct((N_EXPERTS, n), jnp.int32), jax.ShapeDtypeStruct((N_EXPERTS, n), F32)]
    return pl.pallas_call(
        functools.partial(_merge_kernel, route=route, x_starts=x_starts, b_starts=b_starts),
        grid=(n // tm,),
        in_specs=in_specs,
        out_specs=out_specs,
        out_shape=out_shape,
        compiler_params=_cparams(("arbitrary",)),
        name="merge_route" if route else "merge",
    )(*args)


def _ffn_kernel(*refs, residual):
    if residual:
        te_ref, src_ref, nv_ref, h_ref, x_ref, wg_ref, wu_ref, wd_ref, o_ref = refs
    else:
        te_ref, src_ref, nv_ref, h_ref, wg_ref, wu_ref, wd_ref, o_ref = refs
    del te_ref, src_ref
    d_ff = wg_ref.shape[-1]

    @pl.when(pl.program_id(0) < nv_ref[0])
    def _():
        h = h_ref[...].astype(BF16)
        acc = x_ref[...] if residual else None
        for c in range(d_ff // FFN_CHUNK):
            sl = slice(c * FFN_CHUNK, (c + 1) * FFN_CHUNK)
            g = jnp.dot(h, wg_ref[:, sl], preferred_element_type=F32)
            u = jnp.dot(h, wu_ref[:, sl], preferred_element_type=F32)
            act = ((g * jax.nn.sigmoid(g)) * u).astype(BF16)
            y = jnp.dot(act, wd_ref[sl, :], preferred_element_type=F32)
            acc = y if acc is None else acc + y
        o_ref[...] = acc

    @pl.when(pl.program_id(0) >= nv_ref[0])
    def _():
        o_ref[...] = jnp.zeros(o_ref.shape, o_ref.dtype)


def _ffn(h, x, wg, wu, wd, tile_expert, tile_src, n_valid):
    n = h.shape[0]
    tm = TOKEN_TILE
    d_ff = wg.shape[-1]
    assert d_ff % FFN_CHUNK == 0
    row = lambda i, te, src, nv: (src[i], 0)
    wspec = lambda shp: pl.BlockSpec((None,) + shp, lambda i, te, src, nv: (te[i], 0, 0),
                                     pipeline_mode=pl.Buffered(1))
    residual = x is not None
    in_specs = [pl.BlockSpec((tm, D_MODEL), row)]
    args = [h]
    if residual:
        in_specs.append(pl.BlockSpec((tm, D_MODEL), row))
        args.append(x)
    in_specs += [wspec((D_MODEL, d_ff)), wspec((D_MODEL, d_ff)), wspec((d_ff, D_MODEL))]
    args += [wg, wu, wd]
    return pl.pallas_call(
        functools.partial(_ffn_kernel, residual=residual),
        grid_spec=pltpu.PrefetchScalarGridSpec(
            num_scalar_prefetch=3, grid=(n // tm,), in_specs=in_specs,
            out_specs=pl.BlockSpec((tm, D_MODEL), lambda i, te, src, nv: (i, 0))),
        out_shape=jax.ShapeDtypeStruct((n, D_MODEL), F32),
        compiler_params=_cparams(("arbitrary",)),
        name="ffn" if residual else "expert_ffn",
    )(tile_expert, tile_src, n_valid, *args)


def _row_copy(src_ref, src_row, dst_ref, dst_row, sem):
    return pltpu.make_async_copy(src_ref.at[pl.ds(src_row, 1), :], dst_ref.at[pl.ds(dst_row, 1), :], sem)


def _dispatch_kernel(pos_ref, h_ref, zero_ref, xs_ref, sem):
    del zero_ref
    tm = h_ref.shape[0]

    def issue(r, c):
        for kk in range(TOP_K):
            _row_copy(h_ref, r, xs_ref, pos_ref[TOP_K * r + kk], sem).start()
        return c

    lax.fori_loop(0, tm, issue, 0, unroll=ROW_DMA_UNROLL)
    for _ in range(TOP_K):
        pltpu.make_async_copy(h_ref, xs_ref.at[pl.ds(0, tm), :], sem).wait()


def _dispatch(h, pos_flat, n_slots):
    n = h.shape[0]
    tm = TOKEN_TILE
    zeros = jnp.zeros((n_slots, D_MODEL), F32)
    return pl.pallas_call(
        _dispatch_kernel,
        grid=(n // tm,),
        in_specs=[
            pl.BlockSpec((TOP_K * tm,), lambda i: (i,), memory_space=pltpu.SMEM),
            pl.BlockSpec((tm, D_MODEL), lambda i: (i, 0)),
            pl.BlockSpec(memory_space=pl.ANY),
        ],
        out_specs=pl.BlockSpec(memory_space=pl.ANY),
        out_shape=jax.ShapeDtypeStruct((n_slots, D_MODEL), F32),
        scratch_shapes=[pltpu.SemaphoreType.DMA(())],
        input_output_aliases={2: 0},
        compiler_params=_cparams(("arbitrary",)),
        name="dispatch",
    )(pos_flat, h, zeros)


def _combine_kernel(pos_ref, pos_next_ref, x_ref, gate_ref, ys_ref, *rest, out_starts):
    n_out = len(out_starts) - 1
    o_refs, (buf_ref, sem) = rest[:n_out], rest[n_out:]
    tm = x_ref.shape[0]
    i = pl.program_id(0)
    slot = lax.rem(i, 2)

    def gather(p_ref, s):
        def issue(r, c):
            for kk in range(TOP_K):
                _row_copy(ys_ref, p_ref[TOP_K * r + kk], buf_ref.at[s, kk], r, sem.at[s]).start()
            return c

        lax.fori_loop(0, tm, issue, 0, unroll=ROW_DMA_UNROLL)

    @pl.when(i == 0)
    def _():
        gather(pos_ref, 0)

    @pl.when(i + 1 < pl.num_programs(0))
    def _():
        gather(pos_next_ref, 1 - slot)

    for kk in range(TOP_K):
        pltpu.make_async_copy(ys_ref.at[pl.ds(0, tm), :], buf_ref.at[slot, kk], sem.at[slot]).wait()
    gate = gate_ref[...].T
    out = gate[:, 0:1] * buf_ref[slot, 0] + gate[:, 1:2] * buf_ref[slot, 1]
    _write_parts(i, o_refs, out_starts, x_ref[...] + out)


def _combine(x, gate_t, ys, pos_flat, out_rows):
    n = x.shape[0]
    tm = TOKEN_TILE
    n_t = n // tm
    outs = [jax.ShapeDtypeStruct((r, D_MODEL), F32) for r in out_rows]
    out_starts = _part_starts(outs, tm)
    assert out_starts[-1] == n_t
    return pl.pallas_call(
        functools.partial(_combine_kernel, out_starts=out_starts),
        grid=(n_t,),
        in_specs=[
            pl.BlockSpec((TOP_K * tm,), lambda i: (i,), memory_space=pltpu.SMEM),
            pl.BlockSpec((TOP_K * tm,), lambda i: (jnp.minimum(i + 1, n_t - 1),), memory_space=pltpu.SMEM),
            pl.BlockSpec((tm, D_MODEL), lambda i: (i, 0)),
            pl.BlockSpec((N_EXPERTS, tm), lambda i: (0, i)),
            pl.BlockSpec(memory_space=pl.ANY),
        ],
        out_specs=_part_specs(outs, tm, D_MODEL),
        out_shape=outs,
        scratch_shapes=[pltpu.VMEM((2, TOP_K, tm, D_MODEL), F32), pltpu.SemaphoreType.DMA((2,))],
        compiler_params=_cparams(("arbitrary",)),
        name="combine",
    )(pos_flat, pos_flat, x, gate_t, ys)


def _moe(x, h, idx, gate, wg, wu, wd, out_rows):
    n = x.shape[0]
    tm = TOKEN_TILE
    n_tiles = TOP_K * n // tm + N_EXPERTS
    n_slots = n_tiles * tm
    first = idx[0] < idx[1]
    e_lo = jnp.where(first, idx[0], idx[1])
    e_hi = jnp.where(first, idx[1], idx[0])
    g_lo = jnp.where(first, gate[0], gate[1])
    g_hi = jnp.where(first, gate[1], gate[0])
    experts = jnp.arange(N_EXPERTS, dtype=jnp.int32)
    sel = (e_lo[None, :] == experts[:, None]) | (e_hi[None, :] == experts[:, None])
    sel_b = sel.reshape(N_EXPERTS, n // tm, tm).astype(BF16)
    tri = (jnp.arange(tm)[:, None] <= jnp.arange(tm)[None, :]).astype(BF16)
    within = jnp.einsum("ebj,jt->ebt", sel_b, tri, preferred_element_type=F32).astype(jnp.int32)
    block_tot = within[:, :, -1]
    block_off = jnp.cumsum(block_tot, axis=1) - block_tot
    csum = (within + block_off[:, :, None]).reshape(N_EXPERTS, n)
    counts = csum[:, -1]
    padded = ((counts + tm - 1) // tm) * tm
    ends = jnp.cumsum(padded)
    starts = ends - padded
    slot = starts[:, None] + csum - 1
    pos_lo = jnp.sum(jnp.where(e_lo[None, :] == experts[:, None], slot, 0), axis=0)
    pos_hi = jnp.sum(jnp.where(e_hi[None, :] == experts[:, None], slot, 0), axis=0)
    pos_flat = jnp.stack([pos_lo, pos_hi], axis=1).reshape(-1).astype(jnp.int32)
    n_valid = (ends[-1] // tm).astype(jnp.int32)
    tile_src = jnp.minimum(jnp.arange(n_tiles, dtype=jnp.int32), n_valid - 1)
    tile_expert = jnp.minimum(
        jnp.sum((tile_src[:, None] * tm >= ends[None, :]).astype(jnp.int32), axis=1), N_EXPERTS - 1).astype(jnp.int32)
    gate_t = jnp.concatenate([g_lo[None], g_hi[None], jnp.zeros((N_EXPERTS - TOP_K, n), F32)], axis=0)
    xs = _dispatch(h, pos_flat, n_slots)
    ys = _ffn(xs, None, wg, wu, wd, tile_expert, tile_src, n_valid.reshape(1))
    return _combine(x, gate_t, ys, pos_flat, out_rows)


def _rope_layout(w64):
    z = jnp.zeros(w64.shape[:-1] + (MLA_ROPE // 2,), w64.dtype)
    return jnp.concatenate([w64[..., :MLA_ROPE // 2], z, w64[..., MLA_ROPE // 2:], z], axis=-1)


def _qk_layout(w192):
    return jnp.concatenate([w192[..., :MLA_NOPE], _rope_layout(w192[..., MLA_NOPE:])], axis=-1)


def _layer_weights(l, norm_mix, norm_ffn, w_in, na_q_norm, na_k_norm, mla_q_a_norm, w_uq, mla_kv_a_norm, w_ukv,
                   mla_q_norm, mla_k_norm, na_out_norm, mla_out_norm, w_out):
    o5 = 3 * NA_WIDTH + MLA_Q_LORA + MLA_KV_LORA
    w_in_l = w_in[l]
    w_in_p = jnp.concatenate([w_in_l[:, :o5], _rope_layout(w_in_l[:, o5:])], axis=1)
    w_uq_p = _qk_layout(w_uq[l].reshape(MLA_Q_LORA, MLA_HEADS, MLA_QK)).reshape(MLA_Q_LORA, MLA_HEADS * MLA_QK_PAD)
    grp = jnp.arange(NA_WIDTH) // NA_HEAD_DIM
    row = lambda v: v.reshape(1, -1).astype(F32)
    return {
        "g_mix": row(norm_mix[l]),
        "g_ffn": row(norm_ffn[l]),
        "w_in": w_in_p.astype(BF16),
        "gsum": (grp[:, None] == grp[None, :]).astype(BF16),
        "g_qna": row(jnp.tile(na_q_norm[l], NA_HEADS) * (NA_HEAD_DIM ** -0.5 * LOG2E)),
        "g_kna": row(jnp.tile(na_k_norm[l], NA_HEADS)),
        "g_qa": row(mla_q_a_norm[l]),
        "w_uq": w_uq_p.astype(BF16),
        "g_kva": row(mla_kv_a_norm[l]),
        "w_ukv": w_ukv[l].astype(BF16),
        "g_q": row(_qk_layout(mla_q_norm[l]) * (MLA_QK ** -0.5 * LOG2E)),
        "g_k": row(_qk_layout(mla_k_norm[l])),
        "g_oa": row(na_out_norm[l]),
        "g_ob": row(mla_out_norm[l]),
        "w_out": w_out[l].astype(BF16),
    }


def _rope_tables(s_max):
    inv = ROPE_THETA ** (-jnp.arange(0, MLA_ROPE, 2, dtype=F32) / MLA_ROPE)
    ang = jnp.arange(s_max, dtype=F32)[:, None] * inv[None, :]
    cos, sin = jnp.cos(ang), jnp.sin(ang)
    z = jnp.zeros_like(cos)
    return jnp.concatenate([cos, z, cos, z], axis=1), jnp.concatenate([-sin, z, sin, z], axis=1)


def _trunk(x_parts, segments, norm_mix, norm_ffn, w_in, na_q_norm, na_k_norm, na_rpb, mla_q_a_norm, w_uq,
           mla_kv_a_norm, w_ukv, mla_q_norm, mla_k_norm, na_out_norm, mla_out_norm, w_out, ffn_w_gate, ffn_w_up,
           ffn_w_down, moe_router, moe_w_gate, moe_w_up, moe_w_down):
    x_parts = tuple(x_parts)
    part_rows = tuple(a.shape[0] for a in x_parts)
    assert part_rows == tuple(b * s for b, s in segments)
    n = sum(part_rows)
    tm = TOKEN_TILE
    depth = w_in.shape[0]
    s_max = max(s for _, s in segments)
    cos_t, sin_t = _rope_tables(s_max)
    pos_blk = jnp.asarray([i for b, s in segments for _ in range(b) for i in range(s // tm)], jnp.int32)
    n_tiles = n // tm
    dense_tiles = (jnp.zeros((n_tiles,), jnp.int32), jnp.arange(n_tiles, dtype=jnp.int32),
                   jnp.full((1,), n_tiles, jnp.int32))
    for l in range(depth):
        lw = _layer_weights(l, norm_mix, norm_ffn, w_in, na_q_norm, na_k_norm, mla_q_a_norm, w_uq, mla_kv_a_norm,
                            w_ukv, mla_q_norm, mla_k_norm, na_out_norm, mla_out_norm, w_out)
        qna, kna, vna, qm, km, vm = _proj(x_parts, pos_blk, lw, cos_t, sin_t)
        out_a = _na(qna, kna, vna, _na_bias_table(na_rpb[l]), segments)
        b_parts = []
        tok = 0
        for b, s in segments:
            b_parts.append(_flash(qm, km, vm, tok, b, s))
            tok += b * s
        i = l // 2
        if l % 2 == 0:
            x, h = _merge(x_parts, out_a, b_parts, lw, route=False)
            x = _ffn(h, x, ffn_w_gate[i:i + 1].astype(BF16), ffn_w_up[i:i + 1].astype(BF16),
                     ffn_w_down[i:i + 1].astype(BF16), *dense_tiles)
        else:
            wr = jnp.pad(moe_router[i], ((0, 0), (0, LANES - N_EXPERTS)))
            lw["w_router"] = jnp.concatenate(_split_bf16(wr), axis=1)
            x, h, idx, gate = _merge(x_parts, out_a, b_parts, lw, route=True)
            out_rows = part_rows if l == depth - 1 else (n,)
            x_parts = _moe(x, h, idx[:TOP_K], gate[:TOP_K], moe_w_gate[i].astype(BF16), moe_w_up[i].astype(BF16),
                           moe_w_down[i].astype(BF16), out_rows)
            continue
        x_parts = (x,)
    if len(x_parts) == len(part_rows):
        return tuple(x_parts)
    (x,) = x_parts
    splits = np.cumsum((0,) + part_rows)
    return tuple(x[a:b] for a, b in zip(splits[:-1], splits[1:]))


def kernel(x_prompt, x_sample, norm_mix, norm_ffn, w_in, na_q_norm, na_k_norm, na_rpb, mla_q_a_norm, w_uq,
           mla_kv_a_norm, w_ukv, mla_q_norm, mla_k_norm, na_out_norm, mla_out_norm, w_out, ffn_w_gate, ffn_w_up,
           ffn_w_down, moe_router, moe_w_gate, moe_w_up, moe_w_down):
    bp, sp, d = x_prompt.shape
    bs, ss, _ = x_sample.shape
    segments = ((bp, sp), (bs, ss))
    y_p, y_s = _trunk((x_prompt.reshape(bp * sp, d), x_sample.reshape(bs * ss, d)), segments, norm_mix, norm_ffn, w_in, na_q_norm, na_k_norm, na_rpb, mla_q_a_norm, w_uq,
               mla_kv_a_norm, w_ukv, mla_q_norm, mla_k_norm, na_out_norm, mla_out_norm, w_out, ffn_w_gate,
               ffn_w_up, ffn_w_down, moe_router, moe_w_gate, moe_w_up, moe_w_down)
    return y_p.reshape(bp, sp, d), y_s.reshape(bs, ss, d)
```

```python
import functools
import math

import jax
import jax.numpy as jnp
import numpy as np
from jax import lax
from jax.experimental import pallas as pl
from jax.experimental.pallas import tpu as pltpu

F32 = jnp.float32
BF16 = jnp.bfloat16

D_MODEL = 1024
GRID_W = 64
NA_HEADS = 8
NA_HEAD_DIM = 64
NA_ROWS = 8
NA_COLS = 16
NA_WIDTH = NA_HEADS * NA_HEAD_DIM
MLA_HEADS = 4
MLA_Q_LORA = 384
MLA_KV_LORA = 256
MLA_NOPE = 128
MLA_ROPE = 64
MLA_V = 128
MLA_QK = MLA_NOPE + MLA_ROPE
MLA_WIDTH = MLA_HEADS * MLA_V
MLA_QK_PAD = 256
ROPE_THETA = 10000.0
N_EXPERTS = 8
TOP_K = 2
EPS = 1e-6
NEG_INF = -1e30
LOG2E = math.log2(math.e)

LANES = 128
TOKEN_TILE = 512
NA_BLOCK_ROWS = 8
NA_WINDOW_ROWS = 16
NA_ROW_UNROLL = 8
FLASH_TQ = 1024
FLASH_TK = TOKEN_TILE
FLASH_CHUNKS_PER_TRIP = 8
ROW_DMA_UNROLL = 8
FFN_CHUNK = 1408
VMEM_LIMIT = 56 * 1024 * 1024


def _cparams(sem):
    return pltpu.CompilerParams(dimension_semantics=sem, vmem_limit_bytes=VMEM_LIMIT)


def _const_spec(shape):
    nd = len(shape)
    return pl.BlockSpec(shape, lambda *_: (0,) * nd, pipeline_mode=pl.Buffered(1))


def _rms(xf, g):
    y = xf * lax.rsqrt(jnp.mean(xf * xf, axis=-1, keepdims=True) + EPS)
    return y * g


def _split_bf16(v):
    hi = v.astype(BF16)
    lo = (v - hi.astype(F32)).astype(BF16)
    return hi, lo


def _part_starts(parts, tm):
    starts = [0]
    for a in parts:
        assert a.shape[0] % tm == 0
        starts.append(starts[-1] + a.shape[0] // tm)
    return tuple(starts)


def _part_index(i, *_, lo, nt):
    return (jnp.clip(i - lo, 0, nt - 1), 0)


def _part_specs(parts, tm, width):
    starts = _part_starts(parts, tm)
    return [pl.BlockSpec((tm, width), functools.partial(_part_index, lo=lo, nt=hi - lo))
            for lo, hi in zip(starts[:-1], starts[1:])]


def _read_parts(i, refs, starts):
    v = refs[0][...]
    for r, lo in zip(refs[1:], starts[1:]):
        v = jnp.where(i >= lo, r[...], v)
    return v


def _write_parts(i, refs, starts, v):
    if len(refs) == 1:
        refs[0][...] = v
        return
    for r, lo, hi in zip(refs, starts[:-1], starts[1:]):
        @pl.when(jnp.logical_and(i >= lo, i < hi))
        def _(r=r):
            r[...] = v


def _proj_kernel(pos_ref, *refs, x_starts):
    del pos_ref
    n_x = len(x_starts) - 1
    x_refs = refs[:n_x]
    (gmix_ref, win_ref, gsum_ref, gqna_ref, gkna_ref, gqa_ref, wuq_ref, gkva_ref, wukv_ref, gq_ref, gk_ref,
     cos_ref, sin_ref, qna_ref, kna_ref, vna_ref, qm_ref, km_ref, vm_ref) = refs[n_x:]
    h = _rms(_read_parts(pl.program_id(0), x_refs, x_starts), gmix_ref[...]).astype(BF16)

    def proj(lo, hi):
        return jnp.dot(h, win_ref[:, lo:hi], preferred_element_type=F32)

    gsum = gsum_ref[...]

    def head_norm(v, gain):
        ss = jnp.dot((v * v).astype(BF16), gsum, preferred_element_type=F32)
        return (v * lax.rsqrt(ss * (1.0 / NA_HEAD_DIM) + EPS)) * gain

    o1, o2, o3 = NA_WIDTH, 2 * NA_WIDTH, 3 * NA_WIDTH
    o4 = o3 + MLA_Q_LORA
    o5 = o4 + MLA_KV_LORA
    cos = cos_ref[...]
    sin = sin_ref[...]

    def rope(u):
        return u * cos + pltpu.roll(u, 64, 1) * sin

    cq = _rms(proj(o3, o4), gqa_ref[...]).astype(BF16)
    q_all = jnp.dot(cq, wuq_ref[...], preferred_element_type=F32)
    gq = gq_ref[...]
    for hd in range(MLA_HEADS):
        qh = q_all[:, hd * MLA_QK_PAD:(hd + 1) * MLA_QK_PAD]
        r = lax.rsqrt(jnp.sum(qh * qh, axis=-1, keepdims=True) * (1.0 / MLA_QK) + EPS)
        qn = (qh * r) * gq
        qm_ref[hd, 0:MLA_NOPE, :] = qn[:, 0:MLA_NOPE].T.astype(BF16)
        qm_ref[hd, MLA_NOPE:MLA_QK_PAD, :] = rope(qn[:, MLA_NOPE:MLA_QK_PAD]).T.astype(BF16)

    ckv = _rms(proj(o4, o5), gkva_ref[...]).astype(BF16)
    kv = jnp.dot(ckv, wukv_ref[...], preferred_element_type=F32)
    kpe = proj(o5, o5 + LANES)
    ss_pe = jnp.sum(kpe * kpe, axis=-1, keepdims=True)
    gk = gk_ref[...]
    for hd in range(MLA_HEADS):
        base = hd * (MLA_NOPE + MLA_V)
        kn = kv[:, base:base + MLA_NOPE]
        r = lax.rsqrt((jnp.sum(kn * kn, axis=-1, keepdims=True) + ss_pe) * (1.0 / MLA_QK) + EPS)
        km_ref[hd, :, 0:MLA_NOPE] = ((kn * r) * gk[:, 0:MLA_NOPE]).astype(BF16)
        km_ref[hd, :, MLA_NOPE:MLA_QK_PAD] = rope((kpe * r) * gk[:, MLA_NOPE:MLA_QK_PAD]).astype(BF16)
        vm_ref[hd, 0] = kv[:, base + MLA_NOPE:base + MLA_NOPE + MLA_V].T.astype(BF16)

    qna_ref[...] = head_norm(proj(0, o1), gqna_ref[...]).astype(BF16)
    kna_ref[...] = head_norm(proj(o1, o2), gkna_ref[...]).astype(BF16)
    vna_ref[...] = proj(o2, o3).astype(BF16)


def _proj(x_parts, pos_blk, lw, cos_t, sin_t):
    tm = TOKEN_TILE
    x_starts = _part_starts(x_parts, tm)
    n = x_starts[-1] * tm
    row = lambda i, pos: (i, 0)
    head = lambda i, pos: (0, i, 0)
    in_specs = _part_specs(x_parts, tm, D_MODEL) + [
        _const_spec((1, D_MODEL)),
        _const_spec(lw["w_in"].shape),
        _const_spec((NA_WIDTH, NA_WIDTH)),
        _const_spec((1, NA_WIDTH)),
        _const_spec((1, NA_WIDTH)),
        _const_spec((1, MLA_Q_LORA)),
        _const_spec(lw["w_uq"].shape),
        _const_spec((1, MLA_KV_LORA)),
        _const_spec(lw["w_ukv"].shape),
        _const_spec((1, MLA_QK_PAD)),
        _const_spec((1, MLA_QK_PAD)),
        pl.BlockSpec((tm, LANES), lambda i, pos: (pos[i], 0)),
        pl.BlockSpec((tm, LANES), lambda i, pos: (pos[i], 0)),
    ]
    out_specs = [
        pl.BlockSpec((tm, NA_WIDTH), row),
        pl.BlockSpec((tm, NA_WIDTH), row),
        pl.BlockSpec((tm, NA_WIDTH), row),
        pl.BlockSpec((MLA_HEADS, MLA_QK_PAD, tm), lambda i, pos: (0, 0, i)),
        pl.BlockSpec((MLA_HEADS, tm, MLA_QK_PAD), head),
        pl.BlockSpec((MLA_HEADS, 1, MLA_V, tm), lambda i, pos: (0, i, 0, 0)),
    ]
    out_shape = [
        jax.ShapeDtypeStruct((n, NA_WIDTH), BF16),
        jax.ShapeDtypeStruct((n, NA_WIDTH), BF16),
        jax.ShapeDtypeStruct((n, NA_WIDTH), BF16),
        jax.ShapeDtypeStruct((MLA_HEADS, MLA_QK_PAD, n), BF16),
        jax.ShapeDtypeStruct((MLA_HEADS, n, MLA_QK_PAD), BF16),
        jax.ShapeDtypeStruct((MLA_HEADS, n // tm, MLA_V, tm), BF16),
    ]
    return pl.pallas_call(
        functools.partial(_proj_kernel, x_starts=x_starts),
        grid_spec=pltpu.PrefetchScalarGridSpec(
            num_scalar_prefetch=1, grid=(n // tm,), in_specs=in_specs, out_specs=out_specs),
        out_shape=out_shape,
        compiler_params=_cparams(("arbitrary",)),
        name="proj",
    )(pos_blk, *x_parts, lw["g_mix"], lw["w_in"], lw["gsum"], lw["g_qna"], lw["g_kna"], lw["g_qa"], lw["w_uq"],
      lw["g_kva"], lw["w_ukv"], lw["g_q"], lw["g_k"], cos_t, sin_t)


def _na_kernel(w0_ref, kind_ref, q_ref, k_ref, v_ref, bias_ref, o_ref):
    del w0_ref
    kind = kind_ref[pl.program_id(0)]
    is_first = kind == 0
    is_last = kind == 2
    lane = lax.broadcasted_iota(jnp.int32, (GRID_W, LANES), 1)
    lo_half = lane < NA_HEAD_DIM
    half_rows = NA_ROWS // 2

    def one_row(j, carry):
        sh = j - half_rows
        off = jnp.where(is_first, jnp.maximum(sh, 0), jnp.where(is_last, NA_BLOCK_ROWS + jnp.minimum(sh, 0), j))
        dcls = jnp.where(is_first, jnp.minimum(j, half_rows), jnp.where(is_last, jnp.maximum(j, half_rows), half_rows))
        kstart = pl.multiple_of(off * GRID_W, GRID_W)
        qstart = pl.multiple_of(j * GRID_W, GRID_W)
        scores = []
        for p in range(NA_HEADS // 2):
            cols = slice(p * LANES, (p + 1) * LANES)
            qp = q_ref[pl.ds(qstart, GRID_W), cols]
            kp = k_ref[pl.ds(kstart, NA_ROWS * GRID_W), cols]
            zero = jnp.zeros_like(qp)
            q2 = jnp.concatenate([jnp.where(lo_half, qp, zero), jnp.where(lo_half, zero, qp)], axis=0)
            s = lax.dot_general(q2, kp, (((1,), (1,)), ((), ())), preferred_element_type=F32)
            scores.append(s + bias_ref[dcls, p])
        s = jnp.concatenate(scores, axis=0)
        e = jnp.exp2(s - jnp.max(s, axis=-1, keepdims=True))
        inv_l = 1.0 / jnp.sum(e, axis=-1, keepdims=True)
        pb = e.astype(BF16)
        for p in range(NA_HEADS // 2):
            cols = slice(p * LANES, (p + 1) * LANES)
            rows = slice(2 * p * GRID_W, (2 * p + 2) * GRID_W)
            vp = v_ref[pl.ds(kstart, NA_ROWS * GRID_W), cols]
            o2 = jnp.dot(pb[rows], vp, preferred_element_type=F32) * inv_l[rows]
            o_ref[pl.ds(qstart, GRID_W), cols] = jnp.where(lo_half, o2[0:GRID_W], o2[GRID_W:2 * GRID_W])
        return carry

    lax.fori_loop(0, NA_BLOCK_ROWS, one_row, 0, unroll=NA_ROW_UNROLL)


def _na_bias_table(rpb):
    c = np.arange(GRID_W)
    cs = np.clip(c - NA_COLS // 2, 0, GRID_W - NA_COLS)
    kc = np.arange(GRID_W)
    valid = (kc[None, :] >= cs[:, None]) & (kc[None, :] < cs[:, None] + NA_COLS)
    dc = kc[None, :] - c[:, None] + (NA_COLS - 1)
    onehot = ((np.arange(2 * NA_COLS - 1)[:, None, None] == dc[None]) & valid[None]).astype(np.float32)
    t = jnp.einsum("hrd,dck->hrck", rpb.astype(F32), jnp.asarray(onehot), precision=lax.Precision.HIGHEST)
    t = jnp.where(jnp.asarray(valid)[None, None], t * LOG2E, NEG_INF)
    tab = jnp.stack([t[:, NA_ROWS - 1 - d:2 * NA_ROWS - 1 - d] for d in range(NA_ROWS)], axis=0)
    tab = jnp.transpose(tab, (0, 1, 3, 2, 4))
    return tab.reshape(NA_ROWS, NA_HEADS // 2, 2 * GRID_W, NA_ROWS * GRID_W)


def _na_block_tables(segments):
    w0, kind = [], []
    tok = 0
    for b, s in segments:
        rows = s // GRID_W
        nb = rows // NA_BLOCK_ROWS
        for _ in range(b):
            for i in range(nb):
                r0 = min(max(NA_BLOCK_ROWS * i - NA_ROWS // 2, 0), rows - NA_WINDOW_ROWS)
                w0.append(tok // GRID_W + r0)
                kind.append(0 if i == 0 else (2 if i == nb - 1 else 1))
            tok += s
    return jnp.asarray(w0, jnp.int32), jnp.asarray(kind, jnp.int32)


def _na(q, k, v, bias, segments):
    n = q.shape[0]
    tq = NA_BLOCK_ROWS * GRID_W
    tw = NA_WINDOW_ROWS * GRID_W
    w0, kind = _na_block_tables(segments)
    win = pl.BlockSpec((pl.Element(tw), pl.Element(NA_WIDTH)), lambda i, w0, kind: (w0[i] * GRID_W, 0))
    return pl.pallas_call(
        _na_kernel,
        grid_spec=pltpu.PrefetchScalarGridSpec(
            num_scalar_prefetch=2,
            grid=(n // tq,),
            in_specs=[
                pl.BlockSpec((tq, NA_WIDTH), lambda i, w0, kind: (i, 0)),
                win,
                win,
                pl.BlockSpec(bias.shape, lambda i, w0, kind: (0, 0, 0, 0), pipeline_mode=pl.Buffered(1)),
            ],
            out_specs=pl.BlockSpec((tq, NA_WIDTH), lambda i, w0, kind: (i, 0)),
        ),
        out_shape=jax.ShapeDtypeStruct((n, NA_WIDTH), F32),
        compiler_params=_cparams(("arbitrary",)),
        name="na",
    )(w0, kind, q, k, v, bias)


def _flash_kernel(qt_ref, qt_next_ref, k_ref, vt_ref, o_ref, s_scr, acc_ref, *, n_chunks, tk, per_trip):
    acc_ref[...] = jnp.zeros(acc_ref.shape, F32)
    qt = qt_ref[...]
    tq = qt.shape[1]

    def scores(c, slot, q=None):
        start = pl.multiple_of(c * tk, tk)
        s_scr[slot] = jnp.dot(k_ref[pl.ds(start, tk), :], qt if q is None else q,
                              preferred_element_type=F32)

    def softmax_pv(c, slot, m_prev, l_prev):
        st = s_scr[slot]
        m_new = jnp.maximum(m_prev, jnp.max(st, axis=0, keepdims=True))
        alpha = jnp.exp2(m_prev - m_new)
        pt = jnp.exp2(st - m_new)
        l_new = alpha * l_prev + jnp.sum(pt, axis=0, keepdims=True)
        acc_ref[...] = alpha * acc_ref[...] + jnp.dot(vt_ref[c], pt.astype(BF16), preferred_element_type=F32)
        return m_new, l_new

    @pl.when(pl.program_id(2) == 0)
    def _():
        scores(0, 0)

    def trip(base, m, l, final):
        for u in range(per_trip // 2):
            c0 = base + 2 * u
            scores(c0 + 1, 1)
            m, l = softmax_pv(c0, 0, m, l)
            if final and u == per_trip // 2 - 1:
                scores(0, 0, qt_next_ref[...])
            else:
                scores(c0 + 2, 0)
            m, l = softmax_pv(c0 + 1, 1, m, l)
        return m, l

    n_trips = n_chunks // per_trip
    init = (jnp.full((1, tq), -jnp.inf, F32), jnp.zeros((1, tq), F32))
    m, l = lax.fori_loop(0, n_trips - 1, lambda t, c: trip(per_trip * t, c[0], c[1], False), init)
    _, l = trip(per_trip * (n_trips - 1), m, l, True)
    o_ref[...] = (acc_ref[...] / l).T


def _flash(qt, k, vt, tok_off, b, s):
    n = k.shape[1]
    tq, tk = FLASH_TQ, FLASH_TK
    assert tok_off % s == 0 and s % tq == 0 and s % (2 * tk) == 0 and vt.shape[-1] == tk
    nq = s // tq
    nc = s // tk
    per_trip = math.gcd(nc, FLASH_CHUNKS_PER_TRIP)
    qblk0 = tok_off // tq
    sblk0 = tok_off // s
    return pl.pallas_call(
        functools.partial(_flash_kernel, n_chunks=nc, tk=tk, per_trip=per_trip),
        grid=(b, MLA_HEADS, nq),
        in_specs=[
            pl.BlockSpec((None, MLA_QK_PAD, tq), lambda bi, h, i: (h, 0, qblk0 + bi * nq + i)),
            pl.BlockSpec((None, MLA_QK_PAD, tq),
                         lambda bi, h, i: (h, 0, qblk0 + bi * nq + jnp.minimum(i + 1, nq - 1))),
            pl.BlockSpec((None, s, MLA_QK_PAD), lambda bi, h, i: (h, sblk0 + bi, 0)),
            pl.BlockSpec((None, nc, MLA_V, tk), lambda bi, h, i: (h, sblk0 + bi, 0, 0)),
        ],
        out_specs=pl.BlockSpec((tq, MLA_V), lambda bi, h, i: (bi * nq + i, h)),
        out_shape=jax.ShapeDtypeStruct((b * s, MLA_WIDTH), F32),
        scratch_shapes=[pltpu.VMEM((2, tk, tq), F32), pltpu.VMEM((MLA_V, tq), F32)],
        compiler_params=_cparams(("arbitrary", "arbitrary", "arbitrary")),
        name="flash",
    )(qt, qt, k, vt)


def _merge_kernel(*refs, route, x_starts, b_starts):
    n_x, n_b = len(x_starts) - 1, len(b_starts) - 1
    x_refs, a_ref, b_refs = refs[:n_x], refs[n_x], refs[n_x + 1:n_x + 1 + n_b]
    rest = refs[n_x + 1 + n_b:]
    if route:
        ga_ref, gb_ref, wout_ref, gffn_ref, wr_ref, xo_ref, h_ref, idx_ref, gate_ref = rest
    else:
        ga_ref, gb_ref, wout_ref, gffn_ref, xo_ref, h_ref = rest
    i = pl.program_id(0)
    a = _rms(a_ref[...], ga_ref[...]).astype(BF16)
    b = _rms(_read_parts(i, b_refs, b_starts), gb_ref[...]).astype(BF16)
    y = (jnp.dot(a, wout_ref[0:NA_WIDTH, :], preferred_element_type=F32)
         + jnp.dot(b, wout_ref[NA_WIDTH:NA_WIDTH + MLA_WIDTH, :], preferred_element_type=F32))
    xn = _read_parts(i, x_refs, x_starts) + y
    xo_ref[...] = xn
    hf = _rms(xn, gffn_ref[...])
    h_ref[...] = hf.astype(h_ref.dtype)
    if route:
        hh, hl = _split_bf16(hf)
        r1 = jnp.dot(hh, wr_ref[...], preferred_element_type=F32)
        r2 = jnp.dot(hl, wr_ref[:, 0:LANES], preferred_element_type=F32)
        logits = r1[:, 0:LANES] + (r1[:, LANES:2 * LANES] + r2)
        lg = logits.T[0:N_EXPERTS, :]
        eid = lax.broadcasted_iota(jnp.int32, lg.shape, 0)
        t1 = jnp.max(lg, axis=0, keepdims=True)
        i1 = jnp.min(jnp.where(lg == t1, eid, N_EXPERTS), axis=0, keepdims=True)
        lg2 = jnp.where(eid == i1, -jnp.inf, lg)
        t2 = jnp.max(lg2, axis=0, keepdims=True)
        i2 = jnp.min(jnp.where(lg2 == t2, eid, N_EXPERTS), axis=0, keepdims=True)
        e2 = jnp.exp(t2 - t1)
        den = 1.0 + e2
        idx_ref[...] = jnp.where(eid == 0, i1, jnp.where(eid == 1, i2, 0))
        gate_ref[...] = jnp.where(eid == 0, 1.0 / den, jnp.where(eid == 1, e2 / den, 0.0))


def _merge(x_parts, out_a, b_parts, lw, route):
    tm = TOKEN_TILE
    x_starts = _part_starts(x_parts, tm)
    b_starts = _part_starts(b_parts, tm)
    n = out_a.shape[0]
    assert x_starts[-1] * tm == n and b_starts[-1] * tm == n
    row = lambda i: (i, 0)
    in_specs = _part_specs(x_parts, tm, D_MODEL) + [pl.BlockSpec((tm, NA_WIDTH), row)] + _part_specs(
        b_parts, tm, MLA_WIDTH) + [
        _const_spec((1, NA_WIDTH)),
        _const_spec((1, MLA_WIDTH)),
        _const_spec((NA_WIDTH + MLA_WIDTH, D_MODEL)),
        _const_spec((1, D_MODEL)),
    ]
    args = [*x_parts, out_a, *b_parts, lw["g_oa"], lw["g_ob"], lw["w_out"], lw["g_ffn"]]
    out_specs = [pl.BlockSpec((tm, D_MODEL), row), pl.BlockSpec((tm, D_MODEL), row)]
    out_shape = [jax.ShapeDtypeStruct((n, D_MODEL), F32),
                 jax.ShapeDtypeStruct((n, D_MODEL), F32 if route else BF16)]
    if route:
        in_specs += [_const_spec((D_MODEL, 2 * LANES))]
        args += [lw["w_router"]]
        col = lambda i: (0, i)
        out_specs += [pl.BlockSpec((N_EXPERTS, tm), col), pl.BlockSpec((N_EXPERTS, tm), col)]
        out_shape += [jax.ShapeDtypeStruct((N_EXPERTS, n), jnp.int32), jax.ShapeDtypeStruct((N_EXPERTS, n), F32)]
    return pl.pallas_call(
        functools.partial(_merge_kernel, route=route, x_starts=x_starts, b_starts=b_starts),
        grid=(n // tm,),
        in_specs=in_specs,
        out_specs=out_specs,
        out_shape=out_shape,
        compiler_params=_cparams(("arbitrary",)),
        name="merge_route" if route else "merge",
    )(*args)


def _ffn_kernel(*refs, residual):
    if residual:
        te_ref, src_ref, nv_ref, h_ref, x_ref, wg_ref, wu_ref, wd_ref, o_ref = refs
    else:
        te_ref, src_ref, nv_ref, h_ref, wg_ref, wu_ref, wd_ref, o_ref = refs
    del te_ref, src_ref
    d_ff = wg_ref.shape[-1]

    @pl.when(pl.program_id(0) < nv_ref[0])
    def _():
        h = h_ref[...].astype(BF16)
        acc = x_ref[...] if residual else None
        for c in range(d_ff // FFN_CHUNK):
            sl = slice(c * FFN_CHUNK, (c + 1) * FFN_CHUNK)
            g = jnp.dot(h, wg_ref[:, sl], preferred_element_type=F32)
            u = jnp.dot(h, wu_ref[:, sl], preferred_element_type=F32)
            act = ((g * jax.nn.sigmoid(g)) * u).astype(BF16)
            y = jnp.dot(act, wd_ref[sl, :], preferred_element_type=F32)
            acc = y if acc is None else acc + y
        o_ref[...] = acc

    @pl.when(pl.program_id(0) >= nv_ref[0])
    def _():
        o_ref[...] = jnp.zeros(o_ref.shape, o_ref.dtype)


def _ffn(h, x, wg, wu, wd, tile_expert, tile_src, n_valid):
    n = h.shape[0]
    tm = TOKEN_TILE
    d_ff = wg.shape[-1]
    assert d_ff % FFN_CHUNK == 0
    row = lambda i, te, src, nv: (src[i], 0)
    wspec = lambda shp: pl.BlockSpec((None,) + shp, lambda i, te, src, nv: (te[i], 0, 0),
                                     pipeline_mode=pl.Buffered(1))
    residual = x is not None
    in_specs = [pl.BlockSpec((tm, D_MODEL), row)]
    args = [h]
    if residual:
        in_specs.append(pl.BlockSpec((tm, D_MODEL), row))
        args.append(x)
    in_specs += [wspec((D_MODEL, d_ff)), wspec((D_MODEL, d_ff)), wspec((d_ff, D_MODEL))]
    args += [wg, wu, wd]
    return pl.pallas_call(
        functools.partial(_ffn_kernel, residual=residual),
        grid_spec=pltpu.PrefetchScalarGridSpec(
            num_scalar_prefetch=3, grid=(n // tm,), in_specs=in_specs,
            out_specs=pl.BlockSpec((tm, D_MODEL), lambda i, te, src, nv: (i, 0))),
        out_shape=jax.ShapeDtypeStruct((n, D_MODEL), F32),
        compiler_params=_cparams(("arbitrary",)),
        name="ffn" if residual else "expert_ffn",
    )(tile_expert, tile_src, n_valid, *args)


def _row_copy(src_ref, src_row, dst_ref, dst_row, sem):
    return pltpu.make_async_copy(src_ref.at[pl.ds(src_row, 1), :], dst_ref.at[pl.ds(dst_row, 1), :], sem)


def _dispatch_kernel(meta_ref, pos_ref, h_ref, xs_ref, zero_ref, sem, zsem):
    tm = h_ref.shape[0]
    n_tiles = xs_ref.shape[0] // tm

    @pl.when(pl.program_id(0) == 0)
    def _():
        zero_ref[...] = jnp.zeros(zero_ref.shape, zero_ref.dtype)

        def tile_copy(t):
            return pltpu.make_async_copy(zero_ref, xs_ref.at[pl.ds(pl.multiple_of(t * tm, tm), tm), :], zsem)

        def each(fn):
            lax.fori_loop(meta_ref[2 * N_EXPERTS], n_tiles, lambda t, c: (fn(tile_copy(t)), c)[1], 0)
            for e in range(N_EXPERTS):
                lax.fori_loop(meta_ref[e], meta_ref[N_EXPERTS + e],
                              lambda r, c: (fn(_row_copy(zero_ref, 0, xs_ref, r, zsem)), c)[1], 0)

        each(lambda cp: cp.start())
        each(lambda cp: cp.wait())

    def issue(r, c):
        for kk in range(TOP_K):
            _row_copy(h_ref, r, xs_ref, pos_ref[TOP_K * r + kk], sem).start()
        return c

    lax.fori_loop(0, tm, issue, 0, unroll=ROW_DMA_UNROLL)
    for _ in range(TOP_K):
        pltpu.make_async_copy(h_ref, xs_ref.at[pl.ds(0, tm), :], sem).wait()


def _dispatch(h, pos_flat, n_slots, meta):
    n = h.shape[0]
    tm = TOKEN_TILE
    return pl.pallas_call(
        _dispatch_kernel,
        grid_spec=pltpu.PrefetchScalarGridSpec(
            num_scalar_prefetch=1,
            grid=(n // tm,),
            in_specs=[
                pl.BlockSpec((TOP_K * tm,), lambda i, meta: (i,), memory_space=pltpu.SMEM),
                pl.BlockSpec((tm, D_MODEL), lambda i, meta: (i, 0)),
            ],
            out_specs=pl.BlockSpec(memory_space=pl.ANY),
            scratch_shapes=[pltpu.VMEM((tm, D_MODEL), F32), pltpu.SemaphoreType.DMA(()),
                            pltpu.SemaphoreType.DMA(())],
        ),
        out_shape=jax.ShapeDtypeStruct((n_slots, D_MODEL), F32),
        compiler_params=_cparams(("arbitrary",)),
        name="dispatch",
    )(meta, pos_flat, h)


def _combine_kernel(pos_ref, pos_next_ref, x_ref, gate_ref, ys_ref, *rest, out_starts):
    n_out = len(out_starts) - 1
    o_refs, (buf_ref, sem) = rest[:n_out], rest[n_out:]
    tm = x_ref.shape[0]
    i = pl.program_id(0)
    slot = lax.rem(i, 2)

    def gather(p_ref, s):
        def issue(r, c):
            for kk in range(TOP_K):
                _row_copy(ys_ref, p_ref[TOP_K * r + kk], buf_ref.at[s, kk], r, sem.at[s]).start()
            return c

        lax.fori_loop(0, tm, issue, 0, unroll=ROW_DMA_UNROLL)

    @pl.when(i == 0)
    def _():
        gather(pos_ref, 0)

    @pl.when(i + 1 < pl.num_programs(0))
    def _():
        gather(pos_next_ref, 1 - slot)

    for kk in range(TOP_K):
        pltpu.make_async_copy(ys_ref.at[pl.ds(0, tm), :], buf_ref.at[slot, kk], sem.at[slot]).wait()
    gate = gate_ref[...].T
    out = gate[:, 0:1] * buf_ref[slot, 0] + gate[:, 1:2] * buf_ref[slot, 1]
    _write_parts(i, o_refs, out_starts, x_ref[...] + out)


def _combine(x, gate_t, ys, pos_flat, out_rows):
    n = x.shape[0]
    tm = TOKEN_TILE
    n_t = n // tm
    outs = [jax.ShapeDtypeStruct((r, D_MODEL), F32) for r in out_rows]
    out_starts = _part_starts(outs, tm)
    assert out_starts[-1] == n_t
    return pl.pallas_call(
        functools.partial(_combine_kernel, out_starts=out_starts),
        grid=(n_t,),
        in_specs=[
            pl.BlockSpec((TOP_K * tm,), lambda i: (i,), memory_space=pltpu.SMEM),
            pl.BlockSpec((TOP_K * tm,), lambda i: (jnp.minimum(i + 1, n_t - 1),), memory_space=pltpu.SMEM),
            pl.BlockSpec((tm, D_MODEL), lambda i: (i, 0)),
            pl.BlockSpec((N_EXPERTS, tm), lambda i: (0, i)),
            pl.BlockSpec(memory_space=pl.ANY),
        ],
        out_specs=_part_specs(outs, tm, D_MODEL),
        out_shape=outs,
        scratch_shapes=[pltpu.VMEM((2, TOP_K, tm, D_MODEL), F32), pltpu.SemaphoreType.DMA((2,))],
        compiler_params=_cparams(("arbitrary",)),
        name="combine",
    )(pos_flat, pos_flat, x, gate_t, ys)


def _moe(x, h, idx, gate, wg, wu, wd, out_rows):
    n = x.shape[0]
    tm = TOKEN_TILE
    n_tiles = TOP_K * n // tm + N_EXPERTS
    n_slots = n_tiles * tm
    first = idx[0] < idx[1]
    e_lo = jnp.where(first, idx[0], idx[1])
    e_hi = jnp.where(first, idx[1], idx[0])
    g_lo = jnp.where(first, gate[0], gate[1])
    g_hi = jnp.where(first, gate[1], gate[0])
    experts = jnp.arange(N_EXPERTS, dtype=jnp.int32)
    sel = (e_lo[None, :] == experts[:, None]) | (e_hi[None, :] == experts[:, None])
    sel_b = sel.reshape(N_EXPERTS, n // tm, tm).astype(BF16)
    tri = (jnp.arange(tm)[:, None] <= jnp.arange(tm)[None, :]).astype(BF16)
    within = jnp.einsum("ebj,jt->ebt", sel_b, tri, preferred_element_type=F32).astype(jnp.int32)
    block_tot = within[:, :, -1]
    block_off = jnp.cumsum(block_tot, axis=1) - block_tot
    csum = (within + block_off[:, :, None]).reshape(N_EXPERTS, n)
    counts = csum[:, -1]
    padded = ((counts + tm - 1) // tm) * tm
    ends = jnp.cumsum(padded)
    starts = ends - padded
    slot = starts[:, None] + csum - 1
    pos_lo = jnp.sum(jnp.where(e_lo[None, :] == experts[:, None], slot, 0), axis=0)
    pos_hi = jnp.sum(jnp.where(e_hi[None, :] == experts[:, None], slot, 0), axis=0)
    pos_flat = jnp.stack([pos_lo, pos_hi], axis=1).reshape(-1).astype(jnp.int32)
    n_valid = (ends[-1] // tm).astype(jnp.int32)
    tile_src = jnp.minimum(jnp.arange(n_tiles, dtype=jnp.int32), n_valid - 1)
    tile_expert = jnp.minimum(
        jnp.sum((tile_src[:, None] * tm >= ends[None, :]).astype(jnp.int32), axis=1), N_EXPERTS - 1).astype(jnp.int32)
    gate_t = jnp.concatenate([g_lo[None], g_hi[None], jnp.zeros((N_EXPERTS - TOP_K, n), F32)], axis=0)
    meta = jnp.concatenate([starts + counts, ends, n_valid.reshape(1)]).astype(jnp.int32)
    xs = _dispatch(h, pos_flat, n_slots, meta)
    ys = _ffn(xs, None, wg, wu, wd, tile_expert, tile_src, n_valid.reshape(1))
    return _combine(x, gate_t, ys, pos_flat, out_rows)


def _rope_layout(w64):
    z = jnp.zeros(w64.shape[:-1] + (MLA_ROPE // 2,), w64.dtype)
    return jnp.concatenate([w64[..., :MLA_ROPE // 2], z, w64[..., MLA_ROPE // 2:], z], axis=-1)


def _qk_layout(w192):
    return jnp.concatenate([w192[..., :MLA_NOPE], _rope_layout(w192[..., MLA_NOPE:])], axis=-1)


def _layer_weights(l, norm_mix, norm_ffn, w_in, na_q_norm, na_k_norm, mla_q_a_norm, w_uq, mla_kv_a_norm, w_ukv,
                   mla_q_norm, mla_k_norm, na_out_norm, mla_out_norm, w_out):
    o5 = 3 * NA_WIDTH + MLA_Q_LORA + MLA_KV_LORA
    w_in_l = w_in[l]
    w_in_p = jnp.concatenate([w_in_l[:, :o5], _rope_layout(w_in_l[:, o5:])], axis=1)
    w_uq_p = _qk_layout(w_uq[l].reshape(MLA_Q_LORA, MLA_HEADS, MLA_QK)).reshape(MLA_Q_LORA, MLA_HEADS * MLA_QK_PAD)
    grp = jnp.arange(NA_WIDTH) // NA_HEAD_DIM
    row = lambda v: v.reshape(1, -1).astype(F32)
    return {
        "g_mix": row(norm_mix[l]),
        "g_ffn": row(norm_ffn[l]),
        "w_in": w_in_p.astype(BF16),
        "gsum": (grp[:, None] == grp[None, :]).astype(BF16),
        "g_qna": row(jnp.tile(na_q_norm[l], NA_HEADS) * (NA_HEAD_DIM ** -0.5 * LOG2E)),
        "g_kna": row(jnp.tile(na_k_norm[l], NA_HEADS)),
        "g_qa": row(mla_q_a_norm[l]),
        "w_uq": w_uq_p.astype(BF16),
        "g_kva": row(mla_kv_a_norm[l]),
        "w_ukv": w_ukv[l].astype(BF16),
        "g_q": row(_qk_layout(mla_q_norm[l]) * (MLA_QK ** -0.5 * LOG2E)),
        "g_k": row(_qk_layout(mla_k_norm[l])),
        "g_oa": row(na_out_norm[l]),
        "g_ob": row(mla_out_norm[l]),
        "w_out": w_out[l].astype(BF16),
    }


def _rope_tables(s_max):
    inv = ROPE_THETA ** (-jnp.arange(0, MLA_ROPE, 2, dtype=F32) / MLA_ROPE)
    ang = jnp.arange(s_max, dtype=F32)[:, None] * inv[None, :]
    cos, sin = jnp.cos(ang), jnp.sin(ang)
    z = jnp.zeros_like(cos)
    return jnp.concatenate([cos, z, cos, z], axis=1), jnp.concatenate([-sin, z, sin, z], axis=1)


def _trunk(x_parts, segments, norm_mix, norm_ffn, w_in, na_q_norm, na_k_norm, na_rpb, mla_q_a_norm, w_uq,
           mla_kv_a_norm, w_ukv, mla_q_norm, mla_k_norm, na_out_norm, mla_out_norm, w_out, ffn_w_gate, ffn_w_up,
           ffn_w_down, moe_router, moe_w_gate, moe_w_up, moe_w_down):
    x_parts = tuple(x_parts)
    part_rows = tuple(a.shape[0] for a in x_parts)
    assert part_rows == tuple(b * s for b, s in segments)
    n = sum(part_rows)
    tm = TOKEN_TILE
    depth = w_in.shape[0]
    s_max = max(s for _, s in segments)
    cos_t, sin_t = _rope_tables(s_max)
    pos_blk = jnp.asarray([i for b, s in segments for _ in range(b) for i in range(s // tm)], jnp.int32)
    n_tiles = n // tm
    dense_tiles = (jnp.zeros((n_tiles,), jnp.int32), jnp.arange(n_tiles, dtype=jnp.int32),
                   jnp.full((1,), n_tiles, jnp.int32))
    for l in range(depth):
        lw = _layer_weights(l, norm_mix, norm_ffn, w_in, na_q_norm, na_k_norm, mla_q_a_norm, w_uq, mla_kv_a_norm,
                            w_ukv, mla_q_norm, mla_k_norm, na_out_norm, mla_out_norm, w_out)
        qna, kna, vna, qm, km, vm = _proj(x_parts, pos_blk, lw, cos_t, sin_t)
        out_a = _na(qna, kna, vna, _na_bias_table(na_rpb[l]), segments)
        b_parts = []
        tok = 0
        for b, s in segments:
            b_parts.append(_flash(qm, km, vm, tok, b, s))
            tok += b * s
        i = l // 2
        if l % 2 == 0:
            x, h = _merge(x_parts, out_a, b_parts, lw, route=False)
            x = _ffn(h, x, ffn_w_gate[i:i + 1].astype(BF16), ffn_w_up[i:i + 1].astype(BF16),
                     ffn_w_down[i:i + 1].astype(BF16), *dense_tiles)
        else:
            wr = jnp.pad(moe_router[i], ((0, 0), (0, LANES - N_EXPERTS)))
            lw["w_router"] = jnp.concatenate(_split_bf16(wr), axis=1)
            x, h, idx, gate = _merge(x_parts, out_a, b_parts, lw, route=True)
            out_rows = part_rows if l == depth - 1 else (n,)
            x_parts = _moe(x, h, idx[:TOP_K], gate[:TOP_K], moe_w_gate[i].astype(BF16), moe_w_up[i].astype(BF16),
                           moe_w_down[i].astype(BF16), out_rows)
            continue
        x_parts = (x,)
    if len(x_parts) == len(part_rows):
        return tuple(x_parts)
    (x,) = x_parts
    splits = np.cumsum((0,) + part_rows)
    return tuple(x[a:b] for a, b in zip(splits[:-1], splits[1:]))


def kernel(x_prompt, x_sample, norm_mix, norm_ffn, w_in, na_q_norm, na_k_norm, na_rpb, mla_q_a_norm, w_uq,
           mla_kv_a_norm, w_ukv, mla_q_norm, mla_k_norm, na_out_norm, mla_out_norm, w_out, ffn_w_gate, ffn_w_up,
           ffn_w_down, moe_router, moe_w_gate, moe_w_up, moe_w_down):
    bp, sp, d = x_prompt.shape
    bs, ss, _ = x_sample.shape
    segments = ((bp, sp), (bs, ss))
    y_p, y_s = _trunk((x_prompt.reshape(bp * sp, d), x_sample.reshape(bs * ss, d)), segments, norm_mix, norm_ffn, w_in, na_q_norm, na_k_norm, na_rpb, mla_q_a_norm, w_uq,
               mla_kv_a_norm, w_ukv, mla_q_norm, mla_k_norm, na_out_norm, mla_out_norm, w_out, ffn_w_gate,
               ffn_w_up, ffn_w_down, moe_router, moe_w_gate, moe_w_up, moe_w_down)
    return y_p.reshape(bp, sp, d), y_s.reshape(bs, ss, d)
```

```python
import functools
import math

import jax
import jax.numpy as jnp
import numpy as np
from jax import lax
from jax.experimental import pallas as pl
from jax.experimental.pallas import tpu as pltpu

F32 = jnp.float32
BF16 = jnp.bfloat16

D_MODEL = 1024
GRID_W = 64
NA_HEADS = 8
NA_HEAD_DIM = 64
NA_ROWS = 8
NA_COLS = 16
NA_WIDTH = NA_HEADS * NA_HEAD_DIM
MLA_HEADS = 4
MLA_Q_LORA = 384
MLA_KV_LORA = 256
MLA_NOPE = 128
MLA_ROPE = 64
MLA_V = 128
MLA_QK = MLA_NOPE + MLA_ROPE
MLA_WIDTH = MLA_HEADS * MLA_V
MLA_QK_PAD = 256
ROPE_THETA = 10000.0
N_EXPERTS = 8
TOP_K = 2
EPS = 1e-6
NEG_INF = -1e30
LOG2E = math.log2(math.e)

LANES = 128
MXU_WIDTH = 256
TOKEN_TILE = 512
NA_BLOCK_ROWS = 8
NA_WINDOW_ROWS = 16
NA_ROW_UNROLL = 8
FLASH_TQ = 1024
FLASH_TK = TOKEN_TILE
FLASH_CHUNKS_PER_TRIP = 8
ROW_DMA_UNROLL = 8
FFN_CHUNK = 1408
VMEM_LIMIT = 56 * 1024 * 1024


def _cparams(sem):
    return pltpu.CompilerParams(dimension_semantics=sem, vmem_limit_bytes=VMEM_LIMIT)


def _const_spec(shape):
    nd = len(shape)
    return pl.BlockSpec(shape, lambda *_: (0,) * nd, pipeline_mode=pl.Buffered(1))


def _rms(xf, g):
    y = xf * lax.rsqrt(jnp.mean(xf * xf, axis=-1, keepdims=True) + EPS)
    return y * g


def _split_bf16(v):
    hi = v.astype(BF16)
    lo = (v - hi.astype(F32)).astype(BF16)
    return hi, lo


def _part_starts(parts, tm):
    starts = [0]
    for a in parts:
        assert a.shape[0] % tm == 0
        starts.append(starts[-1] + a.shape[0] // tm)
    return tuple(starts)


def _part_index(i, *_, lo, nt):
    return (jnp.clip(i - lo, 0, nt - 1), 0)


def _part_specs(parts, tm, width):
    starts = _part_starts(parts, tm)
    return [pl.BlockSpec((tm, width), functools.partial(_part_index, lo=lo, nt=hi - lo))
            for lo, hi in zip(starts[:-1], starts[1:])]


def _read_parts(i, refs, starts):
    v = refs[0][...]
    for r, lo in zip(refs[1:], starts[1:]):
        v = jnp.where(i >= lo, r[...], v)
    return v


def _write_parts(i, refs, starts, v):
    if len(refs) == 1:
        refs[0][...] = v
        return
    for r, lo, hi in zip(refs, starts[:-1], starts[1:]):
        @pl.when(jnp.logical_and(i >= lo, i < hi))
        def _(r=r):
            r[...] = v


def _proj_kernel(pos_ref, *refs, x_starts):
    del pos_ref
    n_x = len(x_starts) - 1
    x_refs = refs[:n_x]
    (gmix_ref, win_ref, gsum_ref, gqna_ref, gkna_ref, gqa_ref, wuq_ref, gkva_ref, wukv_ref, gq_ref, gk_ref,
     cos_ref, sin_ref, qna_ref, kna_ref, vna_ref, qm_ref, km_ref, vm_ref) = refs[n_x:]
    h = _rms(_read_parts(pl.program_id(0), x_refs, x_starts), gmix_ref[...]).astype(BF16)

    def proj(lo, hi):
        return jnp.dot(h, win_ref[:, lo:hi], preferred_element_type=F32)

    gsum = gsum_ref[...]

    def head_norm(v, gain):
        v2 = (v * v).astype(BF16)
        w = gsum.shape[0]
        ss = jnp.concatenate([jnp.dot(v2[:, c:c + w], gsum, preferred_element_type=F32)
                              for c in range(0, NA_WIDTH, w)], axis=1)
        return (v * lax.rsqrt(ss * (1.0 / NA_HEAD_DIM) + EPS)) * gain

    o1, o2, o3 = NA_WIDTH, 2 * NA_WIDTH, 3 * NA_WIDTH
    o4 = o3 + MLA_Q_LORA
    o5 = o4 + MLA_KV_LORA
    cos = cos_ref[...]
    sin = sin_ref[...]

    def rope(u):
        return u * cos + pltpu.roll(u, 64, 1) * sin

    cq = _rms(proj(o3, o4), gqa_ref[...]).astype(BF16)
    q_all = jnp.dot(cq, wuq_ref[...], preferred_element_type=F32)
    gq = gq_ref[...]
    for hd in range(MLA_HEADS):
        qh = q_all[:, hd * MLA_QK_PAD:(hd + 1) * MLA_QK_PAD]
        r = lax.rsqrt(jnp.sum(qh * qh, axis=-1, keepdims=True) * (1.0 / MLA_QK) + EPS)
        qn = (qh * r) * gq
        qm_ref[hd, 0:MLA_NOPE, :] = qn[:, 0:MLA_NOPE].T.astype(BF16)
        qm_ref[hd, MLA_NOPE:MLA_QK_PAD, :] = rope(qn[:, MLA_NOPE:MLA_QK_PAD]).T.astype(BF16)

    ckv = _rms(proj(o4, o5), gkva_ref[...]).astype(BF16)
    kv = jnp.dot(ckv, wukv_ref[...], preferred_element_type=F32)
    kpe = proj(o5, o5 + LANES)
    ss_pe = jnp.sum(kpe * kpe, axis=-1, keepdims=True)
    gk = gk_ref[...]
    for hd in range(MLA_HEADS):
        base = hd * (MLA_NOPE + MLA_V)
        kn = kv[:, base:base + MLA_NOPE]
        r = lax.rsqrt((jnp.sum(kn * kn, axis=-1, keepdims=True) + ss_pe) * (1.0 / MLA_QK) + EPS)
        km_ref[hd, :, 0:MLA_NOPE] = ((kn * r) * gk[:, 0:MLA_NOPE]).astype(BF16)
        km_ref[hd, :, MLA_NOPE:MLA_QK_PAD] = rope((kpe * r) * gk[:, MLA_NOPE:MLA_QK_PAD]).astype(BF16)
        vm_ref[hd, 0] = kv[:, base + MLA_NOPE:base + MLA_NOPE + MLA_V].T.astype(BF16)

    qna_ref[...] = head_norm(proj(0, o1), gqna_ref[...]).astype(BF16)
    kna_ref[...] = head_norm(proj(o1, o2), gkna_ref[...]).astype(BF16)
    vna_ref[...] = proj(o2, o3).astype(BF16)


def _proj(x_parts, pos_blk, lw, cos_t, sin_t):
    tm = TOKEN_TILE
    x_starts = _part_starts(x_parts, tm)
    n = x_starts[-1] * tm
    row = lambda i, pos: (i, 0)
    head = lambda i, pos: (0, i, 0)
    in_specs = _part_specs(x_parts, tm, D_MODEL) + [
        _const_spec((1, D_MODEL)),
        _const_spec(lw["w_in"].shape),
        _const_spec((MXU_WIDTH, MXU_WIDTH)),
        _const_spec((1, NA_WIDTH)),
        _const_spec((1, NA_WIDTH)),
        _const_spec((1, MLA_Q_LORA)),
        _const_spec(lw["w_uq"].shape),
        _const_spec((1, MLA_KV_LORA)),
        _const_spec(lw["w_ukv"].shape),
        _const_spec((1, MLA_QK_PAD)),
        _const_spec((1, MLA_QK_PAD)),
        pl.BlockSpec((tm, LANES), lambda i, pos: (pos[i], 0)),
        pl.BlockSpec((tm, LANES), lambda i, pos: (pos[i], 0)),
    ]
    out_specs = [
        pl.BlockSpec((tm, NA_WIDTH), row),
        pl.BlockSpec((tm, NA_WIDTH), row),
        pl.BlockSpec((tm, NA_WIDTH), row),
        pl.BlockSpec((MLA_HEADS, MLA_QK_PAD, tm), lambda i, pos: (0, 0, i)),
        pl.BlockSpec((MLA_HEADS, tm, MLA_QK_PAD), head),
        pl.BlockSpec((MLA_HEADS, 1, MLA_V, tm), lambda i, pos: (0, i, 0, 0)),
    ]
    out_shape = [
        jax.ShapeDtypeStruct((n, NA_WIDTH), BF16),
        jax.ShapeDtypeStruct((n, NA_WIDTH), BF16),
        jax.ShapeDtypeStruct((n, NA_WIDTH), BF16),
        jax.ShapeDtypeStruct((MLA_HEADS, MLA_QK_PAD, n), BF16),
        jax.ShapeDtypeStruct((MLA_HEADS, n, MLA_QK_PAD), BF16),
        jax.ShapeDtypeStruct((MLA_HEADS, n // tm, MLA_V, tm), BF16),
    ]
    return pl.pallas_call(
        functools.partial(_proj_kernel, x_starts=x_starts),
        grid_spec=pltpu.PrefetchScalarGridSpec(
            num_scalar_prefetch=1, grid=(n // tm,), in_specs=in_specs, out_specs=out_specs),
        out_shape=out_shape,
        compiler_params=_cparams(("arbitrary",)),
        name="proj",
    )(pos_blk, *x_parts, lw["g_mix"], lw["w_in"], lw["gsum"], lw["g_qna"], lw["g_kna"], lw["g_qa"], lw["w_uq"],
      lw["g_kva"], lw["w_ukv"], lw["g_q"], lw["g_k"], cos_t, sin_t)


def _na_kernel(w0_ref, kind_ref, q_ref, k_ref, v_ref, bias_ref, o_ref):
    del w0_ref
    kind = kind_ref[pl.program_id(0)]
    is_first = kind == 0
    is_last = kind == 2
    lane = lax.broadcasted_iota(jnp.int32, (GRID_W, LANES), 1)
    lo_half = lane < NA_HEAD_DIM
    half_rows = NA_ROWS // 2

    def one_row(j, carry):
        sh = j - half_rows
        off = jnp.where(is_first, jnp.maximum(sh, 0), jnp.where(is_last, NA_BLOCK_ROWS + jnp.minimum(sh, 0), j))
        dcls = jnp.where(is_first, jnp.minimum(j, half_rows), jnp.where(is_last, jnp.maximum(j, half_rows), half_rows))
        kstart = pl.multiple_of(off * GRID_W, GRID_W)
        qstart = pl.multiple_of(j * GRID_W, GRID_W)
        scores = []
        for p in range(NA_HEADS // 2):
            cols = slice(p * LANES, (p + 1) * LANES)
            qp = q_ref[pl.ds(qstart, GRID_W), cols]
            kp = k_ref[pl.ds(kstart, NA_ROWS * GRID_W), cols]
            zero = jnp.zeros_like(qp)
            q2 = jnp.concatenate([jnp.where(lo_half, qp, zero), jnp.where(lo_half, zero, qp)], axis=0)
            s = lax.dot_general(q2, kp, (((1,), (1,)), ((), ())), preferred_element_type=F32)
            scores.append(s + bias_ref[dcls, p])
        s = jnp.concatenate(scores, axis=0)
        e = jnp.exp2(s - jnp.max(s, axis=-1, keepdims=True))
        inv_l = 1.0 / jnp.sum(e, axis=-1, keepdims=True)
        pb = e.astype(BF16)
        for p in range(NA_HEADS // 2):
            cols = slice(p * LANES, (p + 1) * LANES)
            rows = slice(2 * p * GRID_W, (2 * p + 2) * GRID_W)
            vp = v_ref[pl.ds(kstart, NA_ROWS * GRID_W), cols]
            o2 = jnp.dot(pb[rows], vp, preferred_element_type=F32) * inv_l[rows]
            o_ref[pl.ds(qstart, GRID_W), cols] = jnp.where(lo_half, o2[0:GRID_W], o2[GRID_W:2 * GRID_W])
        return carry

    lax.fori_loop(0, NA_BLOCK_ROWS, one_row, 0, unroll=NA_ROW_UNROLL)


def _na_bias_table(rpb):
    c = np.arange(GRID_W)
    cs = np.clip(c - NA_COLS // 2, 0, GRID_W - NA_COLS)
    kc = np.arange(GRID_W)
    valid = (kc[None, :] >= cs[:, None]) & (kc[None, :] < cs[:, None] + NA_COLS)
    dc = kc[None, :] - c[:, None] + (NA_COLS - 1)
    onehot = ((np.arange(2 * NA_COLS - 1)[:, None, None] == dc[None]) & valid[None]).astype(np.float32)
    t = jnp.einsum("hrd,dck->hrck", rpb.astype(F32), jnp.asarray(onehot), precision=lax.Precision.HIGHEST)
    t = jnp.where(jnp.asarray(valid)[None, None], t * LOG2E, NEG_INF)
    tab = jnp.stack([t[:, NA_ROWS - 1 - d:2 * NA_ROWS - 1 - d] for d in range(NA_ROWS)], axis=0)
    tab = jnp.transpose(tab, (0, 1, 3, 2, 4))
    return tab.reshape(NA_ROWS, NA_HEADS // 2, 2 * GRID_W, NA_ROWS * GRID_W)


def _na_block_tables(segments):
    w0, kind = [], []
    tok = 0
    for b, s in segments:
        rows = s // GRID_W
        nb = rows // NA_BLOCK_ROWS
        for _ in range(b):
            for i in range(nb):
                r0 = min(max(NA_BLOCK_ROWS * i - NA_ROWS // 2, 0), rows - NA_WINDOW_ROWS)
                w0.append(tok // GRID_W + r0)
                kind.append(0 if i == 0 else (2 if i == nb - 1 else 1))
            tok += s
    return jnp.asarray(w0, jnp.int32), jnp.asarray(kind, jnp.int32)


def _na(q, k, v, bias, segments):
    n = q.shape[0]
    tq = NA_BLOCK_ROWS * GRID_W
    tw = NA_WINDOW_ROWS * GRID_W
    w0, kind = _na_block_tables(segments)
    win = pl.BlockSpec((pl.Element(tw), pl.Element(NA_WIDTH)), lambda i, w0, kind: (w0[i] * GRID_W, 0))
    return pl.pallas_call(
        _na_kernel,
        grid_spec=pltpu.PrefetchScalarGridSpec(
            num_scalar_prefetch=2,
            grid=(n // tq,),
            in_specs=[
                pl.BlockSpec((tq, NA_WIDTH), lambda i, w0, kind: (i, 0)),
                win,
                win,
                pl.BlockSpec(bias.shape, lambda i, w0, kind: (0, 0, 0, 0), pipeline_mode=pl.Buffered(1)),
            ],
            out_specs=pl.BlockSpec((tq, NA_WIDTH), lambda i, w0, kind: (i, 0)),
        ),
        out_shape=jax.ShapeDtypeStruct((n, NA_WIDTH), F32),
        compiler_params=_cparams(("arbitrary",)),
        name="na",
    )(w0, kind, q, k, v, bias)


def _flash_kernel(qt_ref, qt_next_ref, k_ref, vt_ref, o_ref, s_scr, acc_ref, *, n_chunks, tk, per_trip):
    acc_ref[...] = jnp.zeros(acc_ref.shape, F32)
    qt = qt_ref[...]
    tq = qt.shape[1]

    def scores(c, slot, q=None):
        start = pl.multiple_of(c * tk, tk)
        s_scr[slot] = jnp.dot(k_ref[pl.ds(start, tk), :], qt if q is None else q,
                              preferred_element_type=F32)

    def softmax_pv(c, slot, m_prev, l_prev):
        st = s_scr[slot]
        m_new = jnp.maximum(m_prev, jnp.max(st, axis=0, keepdims=True))
        alpha = jnp.exp2(m_prev - m_new)
        pt = jnp.exp2(st - m_new)
        l_new = alpha * l_prev + jnp.sum(pt, axis=0, keepdims=True)
        acc_ref[...] = alpha * acc_ref[...] + jnp.dot(vt_ref[c], pt.astype(BF16), preferred_element_type=F32)
        return m_new, l_new

    @pl.when(pl.program_id(2) == 0)
    def _():
        scores(0, 0)

    def trip(base, m, l, final):
        for u in range(per_trip // 2):
            c0 = base + 2 * u
            scores(c0 + 1, 1)
            m, l = softmax_pv(c0, 0, m, l)
            if final and u == per_trip // 2 - 1:
                scores(0, 0, qt_next_ref[...])
            else:
                scores(c0 + 2, 0)
            m, l = softmax_pv(c0 + 1, 1, m, l)
        return m, l

    n_trips = n_chunks // per_trip
    init = (jnp.full((1, tq), -jnp.inf, F32), jnp.zeros((1, tq), F32))
    m, l = lax.fori_loop(0, n_trips - 1, lambda t, c: trip(per_trip * t, c[0], c[1], False), init)
    _, l = trip(per_trip * (n_trips - 1), m, l, True)
    o_ref[...] = (acc_ref[...] / l).T


def _flash(qt, k, vt, tok_off, b, s):
    n = k.shape[1]
    tq, tk = FLASH_TQ, FLASH_TK
    assert tok_off % s == 0 and s % tq == 0 and s % (2 * tk) == 0 and vt.shape[-1] == tk
    nq = s // tq
    nc = s // tk
    per_trip = math.gcd(nc, FLASH_CHUNKS_PER_TRIP)
    qblk0 = tok_off // tq
    sblk0 = tok_off // s
    return pl.pallas_call(
        functools.partial(_flash_kernel, n_chunks=nc, tk=tk, per_trip=per_trip),
        grid=(b, MLA_HEADS, nq),
        in_specs=[
            pl.BlockSpec((None, MLA_QK_PAD, tq), lambda bi, h, i: (h, 0, qblk0 + bi * nq + i)),
            pl.BlockSpec((None, MLA_QK_PAD, tq),
                         lambda bi, h, i: (h, 0, qblk0 + bi * nq + jnp.minimum(i + 1, nq - 1))),
            pl.BlockSpec((None, s, MLA_QK_PAD), lambda bi, h, i: (h, sblk0 + bi, 0)),
            pl.BlockSpec((None, nc, MLA_V, tk), lambda bi, h, i: (h, sblk0 + bi, 0, 0)),
        ],
        out_specs=pl.BlockSpec((tq, MLA_V), lambda bi, h, i: (bi * nq + i, h)),
        out_shape=jax.ShapeDtypeStruct((b * s, MLA_WIDTH), F32),
        scratch_shapes=[pltpu.VMEM((2, tk, tq), F32), pltpu.VMEM((MLA_V, tq), F32)],
        compiler_params=_cparams(("arbitrary", "arbitrary", "arbitrary")),
        name="flash",
    )(qt, qt, k, vt)


def _swiglu(h, wg_ref, wu_ref, wd_ref, acc=None):
    d_ff = wg_ref.shape[-1]
    assert d_ff % FFN_CHUNK == 0
    for c in range(d_ff // FFN_CHUNK):
        sl = slice(c * FFN_CHUNK, (c + 1) * FFN_CHUNK)
        g = jnp.dot(h, wg_ref[:, sl], preferred_element_type=F32)
        u = jnp.dot(h, wu_ref[:, sl], preferred_element_type=F32)
        act = ((g * jax.nn.sigmoid(g)) * u).astype(BF16)
        y = jnp.dot(act, wd_ref[sl, :], preferred_element_type=F32)
        acc = y if acc is None else acc + y
    return acc


def _merge_kernel(*refs, route, x_starts, b_starts):
    n_x, n_b = len(x_starts) - 1, len(b_starts) - 1
    x_refs, a_ref, b_refs = refs[:n_x], refs[n_x], refs[n_x + 1:n_x + 1 + n_b]
    rest = refs[n_x + 1 + n_b:]
    if route:
        ga_ref, gb_ref, wout_ref, gffn_ref, wr_ref, xo_ref, h_ref, idx_ref, gate_ref = rest
    else:
        ga_ref, gb_ref, wout_ref, gffn_ref, wg_ref, wu_ref, wd_ref, xo_ref = rest
    i = pl.program_id(0)
    a = _rms(a_ref[...], ga_ref[...]).astype(BF16)
    b = _rms(_read_parts(i, b_refs, b_starts), gb_ref[...]).astype(BF16)
    y = (jnp.dot(a, wout_ref[0:NA_WIDTH, :], preferred_element_type=F32)
         + jnp.dot(b, wout_ref[NA_WIDTH:NA_WIDTH + MLA_WIDTH, :], preferred_element_type=F32))
    xn = _read_parts(i, x_refs, x_starts) + y
    hf = _rms(xn, gffn_ref[...])
    if not route:
        xo_ref[...] = _swiglu(hf.astype(BF16), wg_ref, wu_ref, wd_ref, acc=xn)
        return
    xo_ref[...] = xn
    h_ref[...] = hf
    hh, hl = _split_bf16(hf)
    r1 = jnp.dot(hh, wr_ref[...], preferred_element_type=F32)
    r2 = jnp.dot(hl, wr_ref[:, 0:LANES], preferred_element_type=F32)
    logits = r1[:, 0:LANES] + (r1[:, LANES:2 * LANES] + r2)
    lg = logits.T[0:N_EXPERTS, :]
    eid = lax.broadcasted_iota(jnp.int32, lg.shape, 0)
    t1 = jnp.max(lg, axis=0, keepdims=True)
    i1 = jnp.min(jnp.where(lg == t1, eid, N_EXPERTS), axis=0, keepdims=True)
    lg2 = jnp.where(eid == i1, -jnp.inf, lg)
    t2 = jnp.max(lg2, axis=0, keepdims=True)
    i2 = jnp.min(jnp.where(lg2 == t2, eid, N_EXPERTS), axis=0, keepdims=True)
    e2 = jnp.exp(t2 - t1)
    den = 1.0 + e2
    idx_ref[...] = jnp.where(eid == 0, i1, jnp.where(eid == 1, i2, 0))
    gate_ref[...] = jnp.where(eid == 0, 1.0 / den, jnp.where(eid == 1, e2 / den, 0.0))


def _merge(x_parts, out_a, b_parts, lw, ffn_weights=None):
    route = ffn_weights is None
    tm = TOKEN_TILE
    x_starts = _part_starts(x_parts, tm)
    b_starts = _part_starts(b_parts, tm)
    n = out_a.shape[0]
    assert x_starts[-1] * tm == n and b_starts[-1] * tm == n
    row = lambda i: (i, 0)
    in_specs = _part_specs(x_parts, tm, D_MODEL) + [pl.BlockSpec((tm, NA_WIDTH), row)] + _part_specs(
        b_parts, tm, MLA_WIDTH) + [
        _const_spec((1, NA_WIDTH)),
        _const_spec((1, MLA_WIDTH)),
        _const_spec((NA_WIDTH + MLA_WIDTH, D_MODEL)),
        _const_spec((1, D_MODEL)),
    ]
    args = [*x_parts, out_a, *b_parts, lw["g_oa"], lw["g_ob"], lw["w_out"], lw["g_ffn"]]
    out_specs = [pl.BlockSpec((tm, D_MODEL), row)]
    out_shape = [jax.ShapeDtypeStruct((n, D_MODEL), F32)]
    if route:
        in_specs += [_const_spec((D_MODEL, 2 * LANES))]
        args += [lw["w_router"]]
        col = lambda i: (0, i)
        out_specs += [pl.BlockSpec((tm, D_MODEL), row), pl.BlockSpec((N_EXPERTS, tm), col),
                      pl.BlockSpec((N_EXPERTS, tm), col)]
        out_shape += [jax.ShapeDtypeStruct((n, D_MODEL), F32), jax.ShapeDtypeStruct((N_EXPERTS, n), jnp.int32),
                      jax.ShapeDtypeStruct((N_EXPERTS, n), F32)]
    else:
        in_specs += [_const_spec(w.shape) for w in ffn_weights]
        args += list(ffn_weights)
    out = pl.pallas_call(
        functools.partial(_merge_kernel, route=route, x_starts=x_starts, b_starts=b_starts),
        grid=(n // tm,),
        in_specs=in_specs,
        out_specs=out_specs,
        out_shape=out_shape,
        compiler_params=_cparams(("arbitrary",)),
        name="merge_route" if route else "merge_ffn",
    )(*args)
    return out if route else out[0]


def _ffn_kernel(te_ref, src_ref, nv_ref, h_ref, wg_ref, wu_ref, wd_ref, o_ref):
    del te_ref, src_ref

    @pl.when(pl.program_id(0) < nv_ref[0])
    def _():
        o_ref[...] = _swiglu(h_ref[...].astype(BF16), wg_ref, wu_ref, wd_ref)

    @pl.when(pl.program_id(0) >= nv_ref[0])
    def _():
        o_ref[...] = jnp.zeros(o_ref.shape, o_ref.dtype)


def _ffn(h, wg, wu, wd, tile_expert, tile_src, n_valid):
    n = h.shape[0]
    tm = TOKEN_TILE
    d_ff = wg.shape[-1]
    wspec = lambda shp: pl.BlockSpec((None,) + shp, lambda i, te, src, nv: (te[i], 0, 0),
                                     pipeline_mode=pl.Buffered(1))
    return pl.pallas_call(
        _ffn_kernel,
        grid_spec=pltpu.PrefetchScalarGridSpec(
            num_scalar_prefetch=3, grid=(n // tm,),
            in_specs=[pl.BlockSpec((tm, D_MODEL), lambda i, te, src, nv: (src[i], 0)),
                      wspec((D_MODEL, d_ff)), wspec((D_MODEL, d_ff)), wspec((d_ff, D_MODEL))],
            out_specs=pl.BlockSpec((tm, D_MODEL), lambda i, te, src, nv: (i, 0))),
        out_shape=jax.ShapeDtypeStruct((n, D_MODEL), F32),
        compiler_params=_cparams(("arbitrary",)),
        name="expert_ffn",
    )(tile_expert, tile_src, n_valid, h, wg, wu, wd)


def _row_copy(src_ref, src_row, dst_ref, dst_row, sem):
    return pltpu.make_async_copy(src_ref.at[pl.ds(src_row, 1), :], dst_ref.at[pl.ds(dst_row, 1), :], sem)


def _dispatch_kernel(meta_ref, pos_ref, h_ref, xs_ref, zero_ref, sem, zsem):
    tm = h_ref.shape[0]
    n_tiles = xs_ref.shape[0] // tm

    @pl.when(pl.program_id(0) == 0)
    def _():
        zero_ref[...] = jnp.zeros(zero_ref.shape, zero_ref.dtype)

        def tile_copy(t):
            return pltpu.make_async_copy(zero_ref, xs_ref.at[pl.ds(pl.multiple_of(t * tm, tm), tm), :], zsem)

        def each(fn):
            lax.fori_loop(meta_ref[2 * N_EXPERTS], n_tiles, lambda t, c: (fn(tile_copy(t)), c)[1], 0)
            for e in range(N_EXPERTS):
                lax.fori_loop(meta_ref[e], meta_ref[N_EXPERTS + e],
                              lambda r, c: (fn(_row_copy(zero_ref, 0, xs_ref, r, zsem)), c)[1], 0)

        each(lambda cp: cp.start())
        each(lambda cp: cp.wait())

    def issue(r, c):
        for kk in range(TOP_K):
            _row_copy(h_ref, r, xs_ref, pos_ref[TOP_K * r + kk], sem).start()
        return c

    lax.fori_loop(0, tm, issue, 0, unroll=ROW_DMA_UNROLL)
    for _ in range(TOP_K):
        pltpu.make_async_copy(h_ref, xs_ref.at[pl.ds(0, tm), :], sem).wait()


def _dispatch(h, pos_flat, n_slots, meta):
    n = h.shape[0]
    tm = TOKEN_TILE
    return pl.pallas_call(
        _dispatch_kernel,
        grid_spec=pltpu.PrefetchScalarGridSpec(
            num_scalar_prefetch=1,
            grid=(n // tm,),
            in_specs=[
                pl.BlockSpec((TOP_K * tm,), lambda i, meta: (i,), memory_space=pltpu.SMEM),
                pl.BlockSpec((tm, D_MODEL), lambda i, meta: (i, 0)),
            ],
            out_specs=pl.BlockSpec(memory_space=pl.ANY),
            scratch_shapes=[pltpu.VMEM((tm, D_MODEL), F32), pltpu.SemaphoreType.DMA(()),
                            pltpu.SemaphoreType.DMA(())],
        ),
        out_shape=jax.ShapeDtypeStruct((n_slots, D_MODEL), F32),
        compiler_params=_cparams(("arbitrary",)),
        name="dispatch",
    )(meta, pos_flat, h)


def _combine_kernel(pos_ref, pos_next_ref, x_ref, gate_ref, ys_ref, *rest, out_starts):
    n_out = len(out_starts) - 1
    o_refs, (buf_ref, sem) = rest[:n_out], rest[n_out:]
    tm = x_ref.shape[0]
    i = pl.program_id(0)
    slot = lax.rem(i, 2)

    def gather(p_ref, s):
        def issue(r, c):
            for kk in range(TOP_K):
                _row_copy(ys_ref, p_ref[TOP_K * r + kk], buf_ref.at[s, kk], r, sem.at[s]).start()
            return c

        lax.fori_loop(0, tm, issue, 0, unroll=ROW_DMA_UNROLL)

    @pl.when(i == 0)
    def _():
        gather(pos_ref, 0)

    @pl.when(i + 1 < pl.num_programs(0))
    def _():
        gather(pos_next_ref, 1 - slot)

    for kk in range(TOP_K):
        pltpu.make_async_copy(ys_ref.at[pl.ds(0, tm), :], buf_ref.at[slot, kk], sem.at[slot]).wait()
    gate = gate_ref[...].T
    out = gate[:, 0:1] * buf_ref[slot, 0] + gate[:, 1:2] * buf_ref[slot, 1]
    _write_parts(i, o_refs, out_starts, x_ref[...] + out)


def _combine(x, gate_t, ys, pos_flat, out_rows):
    n = x.shape[0]
    tm = TOKEN_TILE
    n_t = n // tm
    outs = [jax.ShapeDtypeStruct((r, D_MODEL), F32) for r in out_rows]
    out_starts = _part_starts(outs, tm)
    assert out_starts[-1] == n_t
    return pl.pallas_call(
        functools.partial(_combine_kernel, out_starts=out_starts),
        grid=(n_t,),
        in_specs=[
            pl.BlockSpec((TOP_K * tm,), lambda i: (i,), memory_space=pltpu.SMEM),
            pl.BlockSpec((TOP_K * tm,), lambda i: (jnp.minimum(i + 1, n_t - 1),), memory_space=pltpu.SMEM),
            pl.BlockSpec((tm, D_MODEL), lambda i: (i, 0)),
            pl.BlockSpec((N_EXPERTS, tm), lambda i: (0, i)),
            pl.BlockSpec(memory_space=pl.ANY),
        ],
        out_specs=_part_specs(outs, tm, D_MODEL),
        out_shape=outs,
        scratch_shapes=[pltpu.VMEM((2, TOP_K, tm, D_MODEL), F32), pltpu.SemaphoreType.DMA((2,))],
        compiler_params=_cparams(("arbitrary",)),
        name="combine",
    )(pos_flat, pos_flat, x, gate_t, ys)


def _moe(x, h, idx, gate, wg, wu, wd, out_rows):
    n = x.shape[0]
    tm = TOKEN_TILE
    n_tiles = TOP_K * n // tm + N_EXPERTS
    n_slots = n_tiles * tm
    first = idx[0] < idx[1]
    e_lo = jnp.where(first, idx[0], idx[1])
    e_hi = jnp.where(first, idx[1], idx[0])
    g_lo = jnp.where(first, gate[0], gate[1])
    g_hi = jnp.where(first, gate[1], gate[0])
    experts = jnp.arange(N_EXPERTS, dtype=jnp.int32)
    sel = (e_lo[None, :] == experts[:, None]) | (e_hi[None, :] == experts[:, None])
    sel_b = sel.reshape(N_EXPERTS, n // tm, tm).astype(BF16)
    tri = (jnp.arange(tm)[:, None] <= jnp.arange(tm)[None, :]).astype(BF16)
    within = jnp.einsum("ebj,jt->ebt", sel_b, tri, preferred_element_type=F32).astype(jnp.int32)
    block_tot = within[:, :, -1]
    block_off = jnp.cumsum(block_tot, axis=1) - block_tot
    csum = (within + block_off[:, :, None]).reshape(N_EXPERTS, n)
    counts = csum[:, -1]
    padded = ((counts + tm - 1) // tm) * tm
    ends = jnp.cumsum(padded)
    starts = ends - padded
    slot = starts[:, None] + csum - 1
    pos_lo = jnp.sum(jnp.where(e_lo[None, :] == experts[:, None], slot, 0), axis=0)
    pos_hi = jnp.sum(jnp.where(e_hi[None, :] == experts[:, None], slot, 0), axis=0)
    pos_flat = jnp.stack([pos_lo, pos_hi], axis=1).reshape(-1).astype(jnp.int32)
    n_valid = (ends[-1] // tm).astype(jnp.int32)
    tile_src = jnp.minimum(jnp.arange(n_tiles, dtype=jnp.int32), n_valid - 1)
    tile_expert = jnp.minimum(
        jnp.sum((tile_src[:, None] * tm >= ends[None, :]).astype(jnp.int32), axis=1), N_EXPERTS - 1).astype(jnp.int32)
    gate_t = jnp.concatenate([g_lo[None], g_hi[None], jnp.zeros((N_EXPERTS - TOP_K, n), F32)], axis=0)
    meta = jnp.concatenate([starts + counts, ends, n_valid.reshape(1)]).astype(jnp.int32)
    xs = _dispatch(h, pos_flat, n_slots, meta)
    ys = _ffn(xs, wg, wu, wd, tile_expert, tile_src, n_valid.reshape(1))
    return _combine(x, gate_t, ys, pos_flat, out_rows)


def _rope_layout(w64):
    z = jnp.zeros(w64.shape[:-1] + (MLA_ROPE // 2,), w64.dtype)
    return jnp.concatenate([w64[..., :MLA_ROPE // 2], z, w64[..., MLA_ROPE // 2:], z], axis=-1)


def _qk_layout(w192):
    return jnp.concatenate([w192[..., :MLA_NOPE], _rope_layout(w192[..., MLA_NOPE:])], axis=-1)


def _layer_weights(l, norm_mix, norm_ffn, w_in, na_q_norm, na_k_norm, mla_q_a_norm, w_uq, mla_kv_a_norm, w_ukv,
                   mla_q_norm, mla_k_norm, na_out_norm, mla_out_norm, w_out):
    o5 = 3 * NA_WIDTH + MLA_Q_LORA + MLA_KV_LORA
    w_in_l = w_in[l]
    w_in_p = jnp.concatenate([w_in_l[:, :o5], _rope_layout(w_in_l[:, o5:])], axis=1)
    w_uq_p = _qk_layout(w_uq[l].reshape(MLA_Q_LORA, MLA_HEADS, MLA_QK)).reshape(MLA_Q_LORA, MLA_HEADS * MLA_QK_PAD)
    grp = jnp.arange(MXU_WIDTH) // NA_HEAD_DIM
    row = lambda v: v.reshape(1, -1).astype(F32)
    return {
        "g_mix": row(norm_mix[l]),
        "g_ffn": row(norm_ffn[l]),
        "w_in": w_in_p.astype(BF16),
        "gsum": (grp[:, None] == grp[None, :]).astype(BF16),
        "g_qna": row(jnp.tile(na_q_norm[l], NA_HEADS) * (NA_HEAD_DIM ** -0.5 * LOG2E)),
        "g_kna": row(jnp.tile(na_k_norm[l], NA_HEADS)),
        "g_qa": row(mla_q_a_norm[l]),
        "w_uq": w_uq_p.astype(BF16),
        "g_kva": row(mla_kv_a_norm[l]),
        "w_ukv": w_ukv[l].astype(BF16),
        "g_q": row(_qk_layout(mla_q_norm[l]) * (MLA_QK ** -0.5 * LOG2E)),
        "g_k": row(_qk_layout(mla_k_norm[l])),
        "g_oa": row(na_out_norm[l]),
        "g_ob": row(mla_out_norm[l]),
        "w_out": w_out[l].astype(BF16),
    }


def _rope_tables(s_max):
    inv = ROPE_THETA ** (-jnp.arange(0, MLA_ROPE, 2, dtype=F32) / MLA_ROPE)
    ang = jnp.arange(s_max, dtype=F32)[:, None] * inv[None, :]
    cos, sin = jnp.cos(ang), jnp.sin(ang)
    z = jnp.zeros_like(cos)
    return jnp.concatenate([cos, z, cos, z], axis=1), jnp.concatenate([-sin, z, sin, z], axis=1)


def _trunk(x_parts, segments, norm_mix, norm_ffn, w_in, na_q_norm, na_k_norm, na_rpb, mla_q_a_norm, w_uq,
           mla_kv_a_norm, w_ukv, mla_q_norm, mla_k_norm, na_out_norm, mla_out_norm, w_out, ffn_w_gate, ffn_w_up,
           ffn_w_down, moe_router, moe_w_gate, moe_w_up, moe_w_down):
    x_parts = tuple(x_parts)
    part_rows = tuple(a.shape[0] for a in x_parts)
    assert part_rows == tuple(b * s for b, s in segments)
    n = sum(part_rows)
    tm = TOKEN_TILE
    depth = w_in.shape[0]
    s_max = max(s for _, s in segments)
    cos_t, sin_t = _rope_tables(s_max)
    pos_blk = jnp.asarray([i for b, s in segments for _ in range(b) for i in range(s // tm)], jnp.int32)
    for l in range(depth):
        lw = _layer_weights(l, norm_mix, norm_ffn, w_in, na_q_norm, na_k_norm, mla_q_a_norm, w_uq, mla_kv_a_norm,
                            w_ukv, mla_q_norm, mla_k_norm, na_out_norm, mla_out_norm, w_out)
        qna, kna, vna, qm, km, vm = _proj(x_parts, pos_blk, lw, cos_t, sin_t)
        out_a = _na(qna, kna, vna, _na_bias_table(na_rpb[l]), segments)
        b_parts = []
        tok = 0
        for b, s in segments:
            b_parts.append(_flash(qm, km, vm, tok, b, s))
            tok += b * s
        i = l // 2
        if l % 2 == 0:
            x = _merge(x_parts, out_a, b_parts, lw,
                       (ffn_w_gate[i].astype(BF16), ffn_w_up[i].astype(BF16), ffn_w_down[i].astype(BF16)))
        else:
            wr = jnp.pad(moe_router[i], ((0, 0), (0, LANES - N_EXPERTS)))
            lw["w_router"] = jnp.concatenate(_split_bf16(wr), axis=1)
            x, h, idx, gate = _merge(x_parts, out_a, b_parts, lw)
            out_rows = part_rows if l == depth - 1 else (n,)
            x_parts = _moe(x, h, idx[:TOP_K], gate[:TOP_K], moe_w_gate[i].astype(BF16), moe_w_up[i].astype(BF16),
                           moe_w_down[i].astype(BF16), out_rows)
            continue
        x_parts = (x,)
    if len(x_parts) == len(part_rows):
        return tuple(x_parts)
    (x,) = x_parts
    splits = np.cumsum((0,) + part_rows)
    return tuple(x[a:b] for a, b in zip(splits[:-1], splits[1:]))


def kernel(x_prompt, x_sample, norm_mix, norm_ffn, w_in, na_q_norm, na_k_norm, na_rpb, mla_q_a_norm, w_uq,
           mla_kv_a_norm, w_ukv, mla_q_norm, mla_k_norm, na_out_norm, mla_out_norm, w_out, ffn_w_gate, ffn_w_up,
           ffn_w_down, moe_router, moe_w_gate, moe_w_up, moe_w_down):
    bp, sp, d = x_prompt.shape
    bs, ss, _ = x_sample.shape
    segments = ((bp, sp), (bs, ss))
    y_p, y_s = _trunk((x_prompt.reshape(bp * sp, d), x_sample.reshape(bs * ss, d)), segments, norm_mix, norm_ffn,
                      w_in, na_q_norm, na_k_norm, na_rpb, mla_q_a_norm, w_uq, mla_kv_a_norm, w_ukv, mla_q_norm,
                      mla_k_norm, na_out_norm, mla_out_norm, w_out, ffn_w_gate, ffn_w_up, ffn_w_down, moe_router,
                      moe_w_gate, moe_w_up, moe_w_down)
    return y_p.reshape(bp, sp, d), y_s.reshape(bs, ss, d)
```

```python
import functools
import math

import jax
import jax.numpy as jnp
import numpy as np
from jax import lax
from jax.experimental import pallas as pl
from jax.experimental.pallas import tpu as pltpu

F32 = jnp.float32
BF16 = jnp.bfloat16

D_MODEL = 1024
GRID_W = 64
NA_HEADS = 8
NA_HEAD_DIM = 64
NA_ROWS = 8
NA_COLS = 16
NA_WIDTH = NA_HEADS * NA_HEAD_DIM
MLA_HEADS = 4
MLA_Q_LORA = 384
MLA_KV_LORA = 256
MLA_NOPE = 128
MLA_ROPE = 64
MLA_V = 128
MLA_QK = MLA_NOPE + MLA_ROPE
MLA_WIDTH = MLA_HEADS * MLA_V
MLA_QK_PAD = 256
ROPE_THETA = 10000.0
N_EXPERTS = 8
TOP_K = 2
EPS = 1e-6
NEG_INF = -1e30
LOG2E = math.log2(math.e)

LANES = 128
MXU_WIDTH = 256
TOKEN_TILE = 512
NA_BLOCK_ROWS = 8
NA_WINDOW_ROWS = 16
NA_ROW_UNROLL = 8
FLASH_TQ = 1024
FLASH_TK = TOKEN_TILE
FLASH_CHUNKS_PER_TRIP = 8
ROW_DMA_UNROLL = 8
FFN_CHUNK = 1408
VMEM_LIMIT = 56 * 1024 * 1024


def _cparams(sem):
    return pltpu.CompilerParams(dimension_semantics=sem, vmem_limit_bytes=VMEM_LIMIT)


def _const_spec(shape):
    nd = len(shape)
    return pl.BlockSpec(shape, lambda *_: (0,) * nd, pipeline_mode=pl.Buffered(1))


def _rms(xf, g):
    y = xf * lax.rsqrt(jnp.mean(xf * xf, axis=-1, keepdims=True) + EPS)
    return y * g


def _split_bf16(v):
    hi = v.astype(BF16)
    lo = (v - hi.astype(F32)).astype(BF16)
    return hi, lo


def _part_starts(parts, tm):
    starts = [0]
    for a in parts:
        assert a.shape[0] % tm == 0
        starts.append(starts[-1] + a.shape[0] // tm)
    return tuple(starts)


def _part_index(i, *_, lo, nt):
    return (jnp.clip(i - lo, 0, nt - 1), 0)


def _part_specs(parts, tm, width):
    starts = _part_starts(parts, tm)
    return [pl.BlockSpec((tm, width), functools.partial(_part_index, lo=lo, nt=hi - lo))
            for lo, hi in zip(starts[:-1], starts[1:])]


def _read_parts(i, refs, starts):
    v = refs[0][...]
    for r, lo in zip(refs[1:], starts[1:]):
        v = jnp.where(i >= lo, r[...], v)
    return v


def _write_parts(i, refs, starts, v):
    if len(refs) == 1:
        refs[0][...] = v
        return
    for r, lo, hi in zip(refs, starts[:-1], starts[1:]):
        @pl.when(jnp.logical_and(i >= lo, i < hi))
        def _(r=r):
            r[...] = v


def _proj_kernel(pos_ref, *refs, x_starts):
    del pos_ref
    n_x = len(x_starts) - 1
    x_refs = refs[:n_x]
    (gmix_ref, win_ref, gsum_ref, gqna_ref, gkna_ref, gqa_ref, wuq_ref, gkva_ref, wukv_ref, gq_ref, gk_ref,
     cos_ref, sin_ref, qna_ref, kna_ref, vna_ref, qm_ref, km_ref, vm_ref) = refs[n_x:]
    h = _rms(_read_parts(pl.program_id(0), x_refs, x_starts), gmix_ref[...]).astype(BF16)

    def proj(lo, hi):
        return jnp.dot(h, win_ref[:, lo:hi], preferred_element_type=F32)

    gsum = gsum_ref[...]

    def head_norm(v, gain):
        v2 = (v * v).astype(BF16)
        w = gsum.shape[0]
        ss = jnp.concatenate([jnp.dot(v2[:, c:c + w], gsum, preferred_element_type=F32)
                              for c in range(0, NA_WIDTH, w)], axis=1)
        return (v * lax.rsqrt(ss * (1.0 / NA_HEAD_DIM) + EPS)) * gain

    o1, o2, o3 = NA_WIDTH, 2 * NA_WIDTH, 3 * NA_WIDTH
    o4 = o3 + MLA_Q_LORA
    o5 = o4 + MLA_KV_LORA
    cos = cos_ref[...]
    sin = sin_ref[...]

    def rope(u):
        return u * cos + pltpu.roll(u, 64, 1) * sin

    cq = _rms(proj(o3, o4), gqa_ref[...]).astype(BF16)
    q_all = jnp.dot(cq, wuq_ref[...], preferred_element_type=F32)
    gq = gq_ref[...]
    for hd in range(MLA_HEADS):
        qh = q_all[:, hd * MLA_QK_PAD:(hd + 1) * MLA_QK_PAD]
        r = lax.rsqrt(jnp.sum(qh * qh, axis=-1, keepdims=True) * (1.0 / MLA_QK) + EPS)
        qn = (qh * r) * gq
        qm_ref[hd, 0:MLA_NOPE, :] = qn[:, 0:MLA_NOPE].T.astype(BF16)
        qm_ref[hd, MLA_NOPE:MLA_QK_PAD, :] = rope(qn[:, MLA_NOPE:MLA_QK_PAD]).T.astype(BF16)

    ckv = _rms(proj(o4, o5), gkva_ref[...]).astype(BF16)
    kv = jnp.dot(ckv, wukv_ref[...], preferred_element_type=F32)
    kpe = proj(o5, o5 + LANES)
    ss_pe = jnp.sum(kpe * kpe, axis=-1, keepdims=True)
    gk = gk_ref[...]
    for hd in range(MLA_HEADS):
        base = hd * (MLA_NOPE + MLA_V)
        kn = kv[:, base:base + MLA_NOPE]
        r = lax.rsqrt((jnp.sum(kn * kn, axis=-1, keepdims=True) + ss_pe) * (1.0 / MLA_QK) + EPS)
        km_ref[hd, :, 0:MLA_NOPE] = ((kn * r) * gk[:, 0:MLA_NOPE]).astype(BF16)
        km_ref[hd, :, MLA_NOPE:MLA_QK_PAD] = rope((kpe * r) * gk[:, MLA_NOPE:MLA_QK_PAD]).astype(BF16)
        vm_ref[hd, 0] = kv[:, base + MLA_NOPE:base + MLA_NOPE + MLA_V].T.astype(BF16)

    qna_ref[...] = head_norm(proj(0, o1), gqna_ref[...]).astype(BF16)
    kna_ref[...] = head_norm(proj(o1, o2), gkna_ref[...]).astype(BF16)
    vna_ref[...] = proj(o2, o3).astype(BF16)


def _proj(x_parts, pos_blk, lw, cos_t, sin_t):
    tm = TOKEN_TILE
    x_starts = _part_starts(x_parts, tm)
    n = x_starts[-1] * tm
    row = lambda i, pos: (i, 0)
    head = lambda i, pos: (0, i, 0)
    in_specs = _part_specs(x_parts, tm, D_MODEL) + [
        _const_spec((1, D_MODEL)),
        _const_spec(lw["w_in"].shape),
        _const_spec((MXU_WIDTH, MXU_WIDTH)),
        _const_spec((1, NA_WIDTH)),
        _const_spec((1, NA_WIDTH)),
        _const_spec((1, MLA_Q_LORA)),
        _const_spec(lw["w_uq"].shape),
        _const_spec((1, MLA_KV_LORA)),
        _const_spec(lw["w_ukv"].shape),
        _const_spec((1, MLA_QK_PAD)),
        _const_spec((1, MLA_QK_PAD)),
        pl.BlockSpec((tm, LANES), lambda i, pos: (pos[i], 0)),
        pl.BlockSpec((tm, LANES), lambda i, pos: (pos[i], 0)),
    ]
    out_specs = [
        pl.BlockSpec((tm, NA_WIDTH), row),
        pl.BlockSpec((tm, NA_WIDTH), row),
        pl.BlockSpec((tm, NA_WIDTH), row),
        pl.BlockSpec((MLA_HEADS, MLA_QK_PAD, tm), lambda i, pos: (0, 0, i)),
        pl.BlockSpec((MLA_HEADS, tm, MLA_QK_PAD), head),
        pl.BlockSpec((MLA_HEADS, 1, MLA_V, tm), lambda i, pos: (0, i, 0, 0)),
    ]
    out_shape = [
        jax.ShapeDtypeStruct((n, NA_WIDTH), BF16),
        jax.ShapeDtypeStruct((n, NA_WIDTH), BF16),
        jax.ShapeDtypeStruct((n, NA_WIDTH), BF16),
        jax.ShapeDtypeStruct((MLA_HEADS, MLA_QK_PAD, n), BF16),
        jax.ShapeDtypeStruct((MLA_HEADS, n, MLA_QK_PAD), BF16),
        jax.ShapeDtypeStruct((MLA_HEADS, n // tm, MLA_V, tm), BF16),
    ]
    return pl.pallas_call(
        functools.partial(_proj_kernel, x_starts=x_starts),
        grid_spec=pltpu.PrefetchScalarGridSpec(
            num_scalar_prefetch=1, grid=(n // tm,), in_specs=in_specs, out_specs=out_specs),
        out_shape=out_shape,
        compiler_params=_cparams(("arbitrary",)),
        name="proj",
    )(pos_blk, *x_parts, lw["g_mix"], lw["w_in"], lw["gsum"], lw["g_qna"], lw["g_kna"], lw["g_qa"], lw["w_uq"],
      lw["g_kva"], lw["w_ukv"], lw["g_q"], lw["g_k"], cos_t, sin_t)


def _na_kernel(w0_ref, kind_ref, q_ref, k_ref, v_ref, bias_ref, o_ref):
    del w0_ref
    kind = kind_ref[pl.program_id(0)]
    is_first = kind == 0
    is_last = kind == 2
    lane = lax.broadcasted_iota(jnp.int32, (GRID_W, LANES), 1)
    lo_half = lane < NA_HEAD_DIM
    half_rows = NA_ROWS // 2

    def one_row(j, carry):
        sh = j - half_rows
        off = jnp.where(is_first, jnp.maximum(sh, 0), jnp.where(is_last, NA_BLOCK_ROWS + jnp.minimum(sh, 0), j))
        dcls = jnp.where(is_first, jnp.minimum(j, half_rows), jnp.where(is_last, jnp.maximum(j, half_rows), half_rows))
        kstart = pl.multiple_of(off * GRID_W, GRID_W)
        qstart = pl.multiple_of(j * GRID_W, GRID_W)
        scores = []
        for p in range(NA_HEADS // 2):
            cols = slice(p * LANES, (p + 1) * LANES)
            qp = q_ref[pl.ds(qstart, GRID_W), cols]
            kp = k_ref[pl.ds(kstart, NA_ROWS * GRID_W), cols]
            zero = jnp.zeros_like(qp)
            q2 = jnp.concatenate([jnp.where(lo_half, qp, zero), jnp.where(lo_half, zero, qp)], axis=0)
            s = lax.dot_general(q2, kp, (((1,), (1,)), ((), ())), preferred_element_type=F32)
            scores.append(s + bias_ref[dcls, p])
        s = jnp.concatenate(scores, axis=0)
        e = jnp.exp2(s - jnp.max(s, axis=-1, keepdims=True))
        inv_l = 1.0 / jnp.sum(e, axis=-1, keepdims=True)
        pb = e.astype(BF16)
        for p in range(NA_HEADS // 2):
            cols = slice(p * LANES, (p + 1) * LANES)
            rows = slice(2 * p * GRID_W, (2 * p + 2) * GRID_W)
            vp = v_ref[pl.ds(kstart, NA_ROWS * GRID_W), cols]
            o2 = jnp.dot(pb[rows], vp, preferred_element_type=F32) * inv_l[rows]
            o_ref[pl.ds(qstart, GRID_W), cols] = jnp.where(lo_half, o2[0:GRID_W], o2[GRID_W:2 * GRID_W])
        return carry

    lax.fori_loop(0, NA_BLOCK_ROWS, one_row, 0, unroll=NA_ROW_UNROLL)


def _na_bias_table(rpb):
    c = np.arange(GRID_W)
    cs = np.clip(c - NA_COLS // 2, 0, GRID_W - NA_COLS)
    kc = np.arange(GRID_W)
    valid = (kc[None, :] >= cs[:, None]) & (kc[None, :] < cs[:, None] + NA_COLS)
    dc = kc[None, :] - c[:, None] + (NA_COLS - 1)
    onehot = ((np.arange(2 * NA_COLS - 1)[:, None, None] == dc[None]) & valid[None]).astype(np.float32)
    t = jnp.einsum("hrd,dck->hrck", rpb.astype(F32), jnp.asarray(onehot), precision=lax.Precision.HIGHEST)
    t = jnp.where(jnp.asarray(valid)[None, None], t * LOG2E, NEG_INF)
    tab = jnp.stack([t[:, NA_ROWS - 1 - d:2 * NA_ROWS - 1 - d] for d in range(NA_ROWS)], axis=0)
    tab = jnp.transpose(tab, (0, 1, 3, 2, 4))
    return tab.reshape(NA_ROWS, NA_HEADS // 2, 2 * GRID_W, NA_ROWS * GRID_W)


def _na_block_tables(segments):
    w0, kind = [], []
    tok = 0
    for b, s in segments:
        rows = s // GRID_W
        nb = rows // NA_BLOCK_ROWS
        for _ in range(b):
            for i in range(nb):
                r0 = min(max(NA_BLOCK_ROWS * i - NA_ROWS // 2, 0), rows - NA_WINDOW_ROWS)
                w0.append(tok // GRID_W + r0)
                kind.append(0 if i == 0 else (2 if i == nb - 1 else 1))
            tok += s
    return jnp.asarray(w0, jnp.int32), jnp.asarray(kind, jnp.int32)


def _na(q, k, v, bias, segments):
    n = q.shape[0]
    tq = NA_BLOCK_ROWS * GRID_W
    tw = NA_WINDOW_ROWS * GRID_W
    w0, kind = _na_block_tables(segments)
    win = pl.BlockSpec((pl.Element(tw), pl.Element(NA_WIDTH)), lambda i, w0, kind: (w0[i] * GRID_W, 0))
    return pl.pallas_call(
        _na_kernel,
        grid_spec=pltpu.PrefetchScalarGridSpec(
            num_scalar_prefetch=2,
            grid=(n // tq,),
            in_specs=[
                pl.BlockSpec((tq, NA_WIDTH), lambda i, w0, kind: (i, 0)),
                win,
                win,
                pl.BlockSpec(bias.shape, lambda i, w0, kind: (0, 0, 0, 0), pipeline_mode=pl.Buffered(1)),
            ],
            out_specs=pl.BlockSpec((tq, NA_WIDTH), lambda i, w0, kind: (i, 0)),
        ),
        out_shape=jax.ShapeDtypeStruct((n, NA_WIDTH), F32),
        compiler_params=_cparams(("arbitrary",)),
        name="na",
    )(w0, kind, q, k, v, bias)


def _flash_kernel(qt_ref, qt_next_ref, k_ref, vt_ref, o_ref, s_scr, acc_ref, *, n_chunks, tk, per_trip):
    acc_ref[...] = jnp.zeros(acc_ref.shape, F32)
    qt = qt_ref[...]
    tq = qt.shape[1]

    def scores(c, slot, q=None):
        start = pl.multiple_of(c * tk, tk)
        s_scr[slot] = jnp.dot(k_ref[pl.ds(start, tk), :], qt if q is None else q,
                              preferred_element_type=F32)

    def softmax_pv(c, slot, m_prev, l_prev):
        st = s_scr[slot]
        m_new = jnp.maximum(m_prev, jnp.max(st, axis=0, keepdims=True))
        alpha = jnp.exp2(m_prev - m_new)
        pt = jnp.exp2(st - m_new)
        l_new = alpha * l_prev + jnp.sum(pt, axis=0, keepdims=True)
        acc_ref[...] = alpha * acc_ref[...] + jnp.dot(vt_ref[c], pt.astype(BF16), preferred_element_type=F32)
        return m_new, l_new

    @pl.when(pl.program_id(2) == 0)
    def _():
        scores(0, 0)

    def trip(base, m, l, final):
        for u in range(per_trip // 2):
            c0 = base + 2 * u
            scores(c0 + 1, 1)
            m, l = softmax_pv(c0, 0, m, l)
            if final and u == per_trip // 2 - 1:
                scores(0, 0, qt_next_ref[...])
            else:
                scores(c0 + 2, 0)
            m, l = softmax_pv(c0 + 1, 1, m, l)
        return m, l

    n_trips = n_chunks // per_trip
    init = (jnp.full((1, tq), -jnp.inf, F32), jnp.zeros((1, tq), F32))
    m, l = lax.fori_loop(0, n_trips - 1, lambda t, c: trip(per_trip * t, c[0], c[1], False), init)
    _, l = trip(per_trip * (n_trips - 1), m, l, True)
    o_ref[...] = (acc_ref[...] / l).T


def _flash(qt, k, vt, tok_off, b, s):
    n = k.shape[1]
    tq, tk = FLASH_TQ, FLASH_TK
    assert tok_off % s == 0 and s % tq == 0 and s % (2 * tk) == 0 and vt.shape[-1] == tk
    nq = s // tq
    nc = s // tk
    per_trip = math.gcd(nc, FLASH_CHUNKS_PER_TRIP)
    qblk0 = tok_off // tq
    sblk0 = tok_off // s
    return pl.pallas_call(
        functools.partial(_flash_kernel, n_chunks=nc, tk=tk, per_trip=per_trip),
        grid=(b, MLA_HEADS, nq),
        in_specs=[
            pl.BlockSpec((None, MLA_QK_PAD, tq), lambda bi, h, i: (h, 0, qblk0 + bi * nq + i)),
            pl.BlockSpec((None, MLA_QK_PAD, tq),
                         lambda bi, h, i: (h, 0, qblk0 + bi * nq + jnp.minimum(i + 1, nq - 1))),
            pl.BlockSpec((None, s, MLA_QK_PAD), lambda bi, h, i: (h, sblk0 + bi, 0)),
            pl.BlockSpec((None, nc, MLA_V, tk), lambda bi, h, i: (h, sblk0 + bi, 0, 0)),
        ],
        out_specs=pl.BlockSpec((tq, MLA_V), lambda bi, h, i: (bi * nq + i, h)),
        out_shape=jax.ShapeDtypeStruct((b * s, MLA_WIDTH), F32),
        scratch_shapes=[pltpu.VMEM((2, tk, tq), F32), pltpu.VMEM((MLA_V, tq), F32)],
        compiler_params=_cparams(("arbitrary", "arbitrary", "arbitrary")),
        name="flash",
    )(qt, qt, k, vt)


def _swiglu(h, wg_ref, wu_ref, wd_ref, acc=None):
    d_ff = wg_ref.shape[-1]
    assert d_ff % FFN_CHUNK == 0
    for c in range(d_ff // FFN_CHUNK):
        sl = slice(c * FFN_CHUNK, (c + 1) * FFN_CHUNK)
        g = jnp.dot(h, wg_ref[:, sl], preferred_element_type=F32)
        u = jnp.dot(h, wu_ref[:, sl], preferred_element_type=F32)
        act = ((g * jax.nn.sigmoid(g)) * u).astype(BF16)
        y = jnp.dot(act, wd_ref[sl, :], preferred_element_type=F32)
        acc = y if acc is None else acc + y
    return acc


def _merge_kernel(*refs, route, x_starts, b_starts):
    n_x, n_b = len(x_starts) - 1, len(b_starts) - 1
    x_refs, a_ref, b_refs = refs[:n_x], refs[n_x], refs[n_x + 1:n_x + 1 + n_b]
    rest = refs[n_x + 1 + n_b:]
    if route:
        ga_ref, gb_ref, wout_ref, gffn_ref, wr_ref, xo_ref, h_ref, idx_ref, gate_ref = rest
    else:
        ga_ref, gb_ref, wout_ref, gffn_ref, wg_ref, wu_ref, wd_ref, xo_ref = rest
    i = pl.program_id(0)
    a = _rms(a_ref[...], ga_ref[...]).astype(BF16)
    b = _rms(_read_parts(i, b_refs, b_starts), gb_ref[...]).astype(BF16)
    y = (jnp.dot(a, wout_ref[0:NA_WIDTH, :], preferred_element_type=F32)
         + jnp.dot(b, wout_ref[NA_WIDTH:NA_WIDTH + MLA_WIDTH, :], preferred_element_type=F32))
    xn = _read_parts(i, x_refs, x_starts) + y
    hf = _rms(xn, gffn_ref[...])
    if not route:
        xo_ref[...] = _swiglu(hf.astype(BF16), wg_ref, wu_ref, wd_ref, acc=xn)
        return
    xo_ref[...] = xn
    h_ref[...] = hf
    hh, hl = _split_bf16(hf)
    r1 = jnp.dot(hh, wr_ref[...], preferred_element_type=F32)
    r2 = jnp.dot(hl, wr_ref[:, 0:LANES], preferred_element_type=F32)
    logits = r1[:, 0:LANES] + (r1[:, LANES:2 * LANES] + r2)
    lg = logits.T[0:N_EXPERTS, :]
    eid = lax.broadcasted_iota(jnp.int32, lg.shape, 0)
    t1 = jnp.max(lg, axis=0, keepdims=True)
    i1 = jnp.min(jnp.where(lg == t1, eid, N_EXPERTS), axis=0, keepdims=True)
    lg2 = jnp.where(eid == i1, -jnp.inf, lg)
    t2 = jnp.max(lg2, axis=0, keepdims=True)
    i2 = jnp.min(jnp.where(lg2 == t2, eid, N_EXPERTS), axis=0, keepdims=True)
    e2 = jnp.exp(t2 - t1)
    den = 1.0 + e2
    idx_ref[...] = jnp.where(eid == 0, i1, jnp.where(eid == 1, i2, 0))
    gate_ref[...] = jnp.where(eid == 0, 1.0 / den, jnp.where(eid == 1, e2 / den, 0.0))


def _merge(x_parts, out_a, b_parts, lw, ffn_weights=None):
    route = ffn_weights is None
    tm = TOKEN_TILE
    x_starts = _part_starts(x_parts, tm)
    b_starts = _part_starts(b_parts, tm)
    n = out_a.shape[0]
    assert x_starts[-1] * tm == n and b_starts[-1] * tm == n
    row = lambda i: (i, 0)
    in_specs = _part_specs(x_parts, tm, D_MODEL) + [pl.BlockSpec((tm, NA_WIDTH), row)] + _part_specs(
        b_parts, tm, MLA_WIDTH) + [
        _const_spec((1, NA_WIDTH)),
        _const_spec((1, MLA_WIDTH)),
        _const_spec((NA_WIDTH + MLA_WIDTH, D_MODEL)),
        _const_spec((1, D_MODEL)),
    ]
    args = [*x_parts, out_a, *b_parts, lw["g_oa"], lw["g_ob"], lw["w_out"], lw["g_ffn"]]
    out_specs = [pl.BlockSpec((tm, D_MODEL), row)]
    out_shape = [jax.ShapeDtypeStruct((n, D_MODEL), F32)]
    if route:
        in_specs += [_const_spec((D_MODEL, 2 * LANES))]
        args += [lw["w_router"]]
        col = lambda i: (0, i)
        out_specs += [pl.BlockSpec((tm, D_MODEL), row), pl.BlockSpec((N_EXPERTS, tm), col),
                      pl.BlockSpec((N_EXPERTS, tm), col)]
        out_shape += [jax.ShapeDtypeStruct((n, D_MODEL), F32), jax.ShapeDtypeStruct((N_EXPERTS, n), jnp.int32),
                      jax.ShapeDtypeStruct((N_EXPERTS, n), F32)]
    else:
        in_specs += [_const_spec(w.shape) for w in ffn_weights]
        args += list(ffn_weights)
    out = pl.pallas_call(
        functools.partial(_merge_kernel, route=route, x_starts=x_starts, b_starts=b_starts),
        grid=(n // tm,),
        in_specs=in_specs,
        out_specs=out_specs,
        out_shape=out_shape,
        compiler_params=_cparams(("arbitrary",)),
        name="merge_route" if route else "merge_ffn",
    )(*args)
    return out if route else out[0]


def _ffn_kernel(te_ref, src_ref, nv_ref, h_ref, wg_ref, wu_ref, wd_ref, o_ref):
    del te_ref, src_ref

    @pl.when(pl.program_id(0) < nv_ref[0])
    def _():
        o_ref[...] = _swiglu(h_ref[...].astype(BF16), wg_ref, wu_ref, wd_ref)

    @pl.when(pl.program_id(0) >= nv_ref[0])
    def _():
        o_ref[...] = jnp.zeros(o_ref.shape, o_ref.dtype)


def _ffn(h, wg, wu, wd, tile_expert, tile_src, n_valid):
    n = h.shape[0]
    tm = TOKEN_TILE
    d_ff = wg.shape[-1]
    wspec = lambda shp: pl.BlockSpec((None,) + shp, lambda i, te, src, nv: (te[i], 0, 0),
                                     pipeline_mode=pl.Buffered(1))
    return pl.pallas_call(
        _ffn_kernel,
        grid_spec=pltpu.PrefetchScalarGridSpec(
            num_scalar_prefetch=3, grid=(n // tm,),
            in_specs=[pl.BlockSpec((tm, D_MODEL), lambda i, te, src, nv: (src[i], 0)),
                      wspec((D_MODEL, d_ff)), wspec((D_MODEL, d_ff)), wspec((d_ff, D_MODEL))],
            out_specs=pl.BlockSpec((tm, D_MODEL), lambda i, te, src, nv: (i, 0))),
        out_shape=jax.ShapeDtypeStruct((n, D_MODEL), F32),
        compiler_params=_cparams(("arbitrary",)),
        name="expert_ffn",
    )(tile_expert, tile_src, n_valid, h, wg, wu, wd)


def _row_copy(src_ref, src_row, dst_ref, dst_row, sem):
    return pltpu.make_async_copy(src_ref.at[pl.ds(src_row, 1), :], dst_ref.at[pl.ds(dst_row, 1), :], sem)


def _dispatch_kernel(meta_ref, pos_ref, h_ref, xs_ref, zero_ref, stage_ref, sem, zsem):
    tm = h_ref.shape[0]
    n_tiles = xs_ref.shape[0] // tm

    @pl.when(pl.program_id(0) == 0)
    def _():
        zero_ref[...] = jnp.zeros(zero_ref.shape, zero_ref.dtype)

        def tile_copy(t):
            return pltpu.make_async_copy(zero_ref, xs_ref.at[pl.ds(pl.multiple_of(t * tm, tm), tm), :], zsem)

        def each(fn):
            lax.fori_loop(meta_ref[2 * N_EXPERTS], n_tiles, lambda t, c: (fn(tile_copy(t)), c)[1], 0)
            for e in range(N_EXPERTS):
                lax.fori_loop(meta_ref[e], meta_ref[N_EXPERTS + e],
                              lambda r, c: (fn(_row_copy(zero_ref, 0, xs_ref, r, zsem)), c)[1], 0)

        each(lambda cp: cp.start())
        each(lambda cp: cp.wait())

    i = pl.program_id(0)
    n_steps = pl.num_programs(0)
    slot = lax.rem(i, 2)

    def wait_slot(s):
        for _ in range(TOP_K):
            pltpu.make_async_copy(stage_ref.at[s], xs_ref.at[pl.ds(0, tm), :], sem.at[s]).wait()

    @pl.when(i >= 2)
    def _():
        wait_slot(slot)

    stage_ref[slot] = h_ref[...]

    def issue(r, c):
        for kk in range(TOP_K):
            _row_copy(stage_ref.at[slot], r, xs_ref, pos_ref[TOP_K * r + kk], sem.at[slot]).start()
        return c

    lax.fori_loop(0, tm, issue, 0, unroll=ROW_DMA_UNROLL)

    @pl.when(i == n_steps - 1)
    def _():
        wait_slot(slot)

        @pl.when(n_steps > 1)
        def _():
            wait_slot(1 - slot)


def _dispatch(h, pos_flat, n_slots, meta):
    n = h.shape[0]
    tm = TOKEN_TILE
    return pl.pallas_call(
        _dispatch_kernel,
        grid_spec=pltpu.PrefetchScalarGridSpec(
            num_scalar_prefetch=1,
            grid=(n // tm,),
            in_specs=[
                pl.BlockSpec((TOP_K * tm,), lambda i, meta: (i,), memory_space=pltpu.SMEM),
                pl.BlockSpec((tm, D_MODEL), lambda i, meta: (i, 0)),
            ],
            out_specs=pl.BlockSpec(memory_space=pl.ANY),
            scratch_shapes=[pltpu.VMEM((tm, D_MODEL), F32), pltpu.VMEM((2, tm, D_MODEL), F32),
                            pltpu.SemaphoreType.DMA((2,)), pltpu.SemaphoreType.DMA(())],
        ),
        out_shape=jax.ShapeDtypeStruct((n_slots, D_MODEL), F32),
        compiler_params=_cparams(("arbitrary",)),
        name="dispatch",
    )(meta, pos_flat, h)


def _combine_kernel(pos_ref, pos_next_ref, x_ref, gate_ref, ys_ref, *rest, out_starts):
    n_out = len(out_starts) - 1
    o_refs, (buf_ref, sem) = rest[:n_out], rest[n_out:]
    tm = x_ref.shape[0]
    i = pl.program_id(0)
    slot = lax.rem(i, 2)

    def gather(p_ref, s):
        def issue(r, c):
            for kk in range(TOP_K):
                _row_copy(ys_ref, p_ref[TOP_K * r + kk], buf_ref.at[s, kk], r, sem.at[s]).start()
            return c

        lax.fori_loop(0, tm, issue, 0, unroll=ROW_DMA_UNROLL)

    @pl.when(i == 0)
    def _():
        gather(pos_ref, 0)

    @pl.when(i + 1 < pl.num_programs(0))
    def _():
        gather(pos_next_ref, 1 - slot)

    for kk in range(TOP_K):
        pltpu.make_async_copy(ys_ref.at[pl.ds(0, tm), :], buf_ref.at[slot, kk], sem.at[slot]).wait()
    gate = gate_ref[...].T
    out = gate[:, 0:1] * buf_ref[slot, 0] + gate[:, 1:2] * buf_ref[slot, 1]
    _write_parts(i, o_refs, out_starts, x_ref[...] + out)


def _combine(x, gate_t, ys, pos_flat, out_rows):
    n = x.shape[0]
    tm = TOKEN_TILE
    n_t = n // tm
    outs = [jax.ShapeDtypeStruct((r, D_MODEL), F32) for r in out_rows]
    out_starts = _part_starts(outs, tm)
    assert out_starts[-1] == n_t
    return pl.pallas_call(
        functools.partial(_combine_kernel, out_starts=out_starts),
        grid=(n_t,),
        in_specs=[
            pl.BlockSpec((TOP_K * tm,), lambda i: (i,), memory_space=pltpu.SMEM),
            pl.BlockSpec((TOP_K * tm,), lambda i: (jnp.minimum(i + 1, n_t - 1),), memory_space=pltpu.SMEM),
            pl.BlockSpec((tm, D_MODEL), lambda i: (i, 0)),
            pl.BlockSpec((N_EXPERTS, tm), lambda i: (0, i)),
            pl.BlockSpec(memory_space=pl.ANY),
        ],
        out_specs=_part_specs(outs, tm, D_MODEL),
        out_shape=outs,
        scratch_shapes=[pltpu.VMEM((2, TOP_K, tm, D_MODEL), F32), pltpu.SemaphoreType.DMA((2,))],
        compiler_params=_cparams(("arbitrary",)),
        name="combine",
    )(pos_flat, pos_flat, x, gate_t, ys)


def _moe(x, h, idx, gate, wg, wu, wd, out_rows):
    n = x.shape[0]
    tm = TOKEN_TILE
    n_tiles = TOP_K * n // tm + N_EXPERTS
    n_slots = n_tiles * tm
    first = idx[0] < idx[1]
    e_lo = jnp.where(first, idx[0], idx[1])
    e_hi = jnp.where(first, idx[1], idx[0])
    g_lo = jnp.where(first, gate[0], gate[1])
    g_hi = jnp.where(first, gate[1], gate[0])
    experts = jnp.arange(N_EXPERTS, dtype=jnp.int32)
    sel = (e_lo[None, :] == experts[:, None]) | (e_hi[None, :] == experts[:, None])
    sel_b = sel.reshape(N_EXPERTS, n // tm, tm).astype(BF16)
    tri = (jnp.arange(tm)[:, None] <= jnp.arange(tm)[None, :]).astype(BF16)
    within = jnp.einsum("ebj,jt->ebt", sel_b, tri, preferred_element_type=F32).astype(jnp.int32)
    block_tot = within[:, :, -1]
    block_off = jnp.cumsum(block_tot, axis=1) - block_tot
    csum = (within + block_off[:, :, None]).reshape(N_EXPERTS, n)
    counts = csum[:, -1]
    padded = ((counts + tm - 1) // tm) * tm
    ends = jnp.cumsum(padded)
    starts = ends - padded
    slot = starts[:, None] + csum - 1
    pos_lo = jnp.sum(jnp.where(e_lo[None, :] == experts[:, None], slot, 0), axis=0)
    pos_hi = jnp.sum(jnp.where(e_hi[None, :] == experts[:, None], slot, 0), axis=0)
    pos_flat = jnp.stack([pos_lo, pos_hi], axis=1).reshape(-1).astype(jnp.int32)
    n_valid = (ends[-1] // tm).astype(jnp.int32)
    tile_src = jnp.minimum(jnp.arange(n_tiles, dtype=jnp.int32), n_valid - 1)
    tile_expert = jnp.minimum(
        jnp.sum((tile_src[:, None] * tm >= ends[None, :]).astype(jnp.int32), axis=1), N_EXPERTS - 1).astype(jnp.int32)
    gate_t = jnp.concatenate([g_lo[None], g_hi[None], jnp.zeros((N_EXPERTS - TOP_K, n), F32)], axis=0)
    meta = jnp.concatenate([starts + counts, ends, n_valid.reshape(1)]).astype(jnp.int32)
    xs = _dispatch(h, pos_flat, n_slots, meta)
    ys = _ffn(xs, wg, wu, wd, tile_expert, tile_src, n_valid.reshape(1))
    return _combine(x, gate_t, ys, pos_flat, out_rows)


def _rope_layout(w64):
    z = jnp.zeros(w64.shape[:-1] + (MLA_ROPE // 2,), w64.dtype)
    return jnp.concatenate([w64[..., :MLA_ROPE // 2], z, w64[..., MLA_ROPE // 2:], z], axis=-1)


def _qk_layout(w192):
    return jnp.concatenate([w192[..., :MLA_NOPE], _rope_layout(w192[..., MLA_NOPE:])], axis=-1)


def _layer_weights(l, norm_mix, norm_ffn, w_in, na_q_norm, na_k_norm, mla_q_a_norm, w_uq, mla_kv_a_norm, w_ukv,
                   mla_q_norm, mla_k_norm, na_out_norm, mla_out_norm, w_out):
    o5 = 3 * NA_WIDTH + MLA_Q_LORA + MLA_KV_LORA
    w_in_l = w_in[l]
    w_in_p = jnp.concatenate([w_in_l[:, :o5], _rope_layout(w_in_l[:, o5:])], axis=1)
    w_uq_p = _qk_layout(w_uq[l].reshape(MLA_Q_LORA, MLA_HEADS, MLA_QK)).reshape(MLA_Q_LORA, MLA_HEADS * MLA_QK_PAD)
    grp = jnp.arange(MXU_WIDTH) // NA_HEAD_DIM
    row = lambda v: v.reshape(1, -1).astype(F32)
    return {
        "g_mix": row(norm_mix[l]),
        "g_ffn": row(norm_ffn[l]),
        "w_in": w_in_p.astype(BF16),
        "gsum": (grp[:, None] == grp[None, :]).astype(BF16),
        "g_qna": row(jnp.tile(na_q_norm[l], NA_HEADS) * (NA_HEAD_DIM ** -0.5 * LOG2E)),
        "g_kna": row(jnp.tile(na_k_norm[l], NA_HEADS)),
        "g_qa": row(mla_q_a_norm[l]),
        "w_uq": w_uq_p.astype(BF16),
        "g_kva": row(mla_kv_a_norm[l]),
        "w_ukv": w_ukv[l].astype(BF16),
        "g_q": row(_qk_layout(mla_q_norm[l]) * (MLA_QK ** -0.5 * LOG2E)),
        "g_k": row(_qk_layout(mla_k_norm[l])),
        "g_oa": row(na_out_norm[l]),
        "g_ob": row(mla_out_norm[l]),
        "w_out": w_out[l].astype(BF16),
    }


def _rope_tables(s_max):
    inv = ROPE_THETA ** (-jnp.arange(0, MLA_ROPE, 2, dtype=F32) / MLA_ROPE)
    ang = jnp.arange(s_max, dtype=F32)[:, None] * inv[None, :]
    cos, sin = jnp.cos(ang), jnp.sin(ang)
    z = jnp.zeros_like(cos)
    return jnp.concatenate([cos, z, cos, z], axis=1), jnp.concatenate([-sin, z, sin, z], axis=1)


def _trunk(x_parts, segments, norm_mix, norm_ffn, w_in, na_q_norm, na_k_norm, na_rpb, mla_q_a_norm, w_uq,
           mla_kv_a_norm, w_ukv, mla_q_norm, mla_k_norm, na_out_norm, mla_out_norm, w_out, ffn_w_gate, ffn_w_up,
           ffn_w_down, moe_router, moe_w_gate, moe_w_up, moe_w_down):
    x_parts = tuple(x_parts)
    part_rows = tuple(a.shape[0] for a in x_parts)
    assert part_rows == tuple(b * s for b, s in segments)
    n = sum(part_rows)
    tm = TOKEN_TILE
    depth = w_in.shape[0]
    s_max = max(s for _, s in segments)
    cos_t, sin_t = _rope_tables(s_max)
    pos_blk = jnp.asarray([i for b, s in segments for _ in range(b) for i in range(s // tm)], jnp.int32)
    for l in range(depth):
        lw = _layer_weights(l, norm_mix, norm_ffn, w_in, na_q_norm, na_k_norm, mla_q_a_norm, w_uq, mla_kv_a_norm,
                            w_ukv, mla_q_norm, mla_k_norm, na_out_norm, mla_out_norm, w_out)
        qna, kna, vna, qm, km, vm = _proj(x_parts, pos_blk, lw, cos_t, sin_t)
        out_a = _na(qna, kna, vna, _na_bias_table(na_rpb[l]), segments)
        b_parts = []
        tok = 0
        for b, s in segments:
            b_parts.append(_flash(qm, km, vm, tok, b, s))
            tok += b * s
        i = l // 2
        if l % 2 == 0:
            x = _merge(x_parts, out_a, b_parts, lw,
                       (ffn_w_gate[i].astype(BF16), ffn_w_up[i].astype(BF16), ffn_w_down[i].astype(BF16)))
        else:
            wr = jnp.pad(moe_router[i], ((0, 0), (0, LANES - N_EXPERTS)))
            lw["w_router"] = jnp.concatenate(_split_bf16(wr), axis=1)
            x, h, idx, gate = _merge(x_parts, out_a, b_parts, lw)
            out_rows = part_rows if l == depth - 1 else (n,)
            x_parts = _moe(x, h, idx[:TOP_K], gate[:TOP_K], moe_w_gate[i].astype(BF16), moe_w_up[i].astype(BF16),
                           moe_w_down[i].astype(BF16), out_rows)
            continue
        x_parts = (x,)
    if len(x_parts) == len(part_rows):
        return tuple(x_parts)
    (x,) = x_parts
    splits = np.cumsum((0,) + part_rows)
    return tuple(x[a:b] for a, b in zip(splits[:-1], splits[1:]))


def kernel(x_prompt, x_sample, norm_mix, norm_ffn, w_in, na_q_norm, na_k_norm, na_rpb, mla_q_a_norm, w_uq,
           mla_kv_a_norm, w_ukv, mla_q_norm, mla_k_norm, na_out_norm, mla_out_norm, w_out, ffn_w_gate, ffn_w_up,
           ffn_w_down, moe_router, moe_w_gate, moe_w_up, moe_w_down):
    bp, sp, d = x_prompt.shape
    bs, ss, _ = x_sample.shape
    segments = ((bp, sp), (bs, ss))
    y_p, y_s = _trunk((x_prompt.reshape(bp * sp, d), x_sample.reshape(bs * ss, d)), segments, norm_mix, norm_ffn,
                      w_in, na_q_norm, na_k_norm, na_rpb, mla_q_a_norm, w_uq, mla_kv_a_norm, w_ukv, mla_q_norm,
                      mla_k_norm, na_out_norm, mla_out_norm, w_out, ffn_w_gate, ffn_w_up, ffn_w_down, moe_router,
                      moe_w_gate, moe_w_up, moe_w_down)
    return y_p.reshape(bp, sp, d), y_s.reshape(bs, ss, d)
```

```python
import functools
import math

import jax
import jax.numpy as jnp
import numpy as np
from jax import lax
from jax.experimental import pallas as pl
from jax.experimental.pallas import tpu as pltpu

F32 = jnp.float32
BF16 = jnp.bfloat16

D_MODEL = 1024
GRID_W = 64
NA_HEADS = 8
NA_HEAD_DIM = 64
NA_ROWS = 8
NA_COLS = 16
NA_WIDTH = NA_HEADS * NA_HEAD_DIM
MLA_HEADS = 4
MLA_Q_LORA = 384
MLA_KV_LORA = 256
MLA_NOPE = 128
MLA_ROPE = 64
MLA_V = 128
MLA_QK = MLA_NOPE + MLA_ROPE
MLA_WIDTH = MLA_HEADS * MLA_V
MLA_QK_PAD = 256
ROPE_THETA = 10000.0
N_EXPERTS = 8
TOP_K = 2
EPS = 1e-6
NEG_INF = -1e30
LOG2E = math.log2(math.e)

LANES = 128
MXU_WIDTH = 256
TOKEN_TILE = 512
NA_BLOCK_ROWS = 8
NA_WINDOW_ROWS = 16
NA_ROW_UNROLL = 8
FLASH_TQ = 1024
FLASH_TK = TOKEN_TILE
FLASH_CHUNKS_PER_TRIP = 8
ROW_DMA_UNROLL = 8
FFN_CHUNK = 1408
VMEM_LIMIT = 56 * 1024 * 1024


def _cparams(sem):
    return pltpu.CompilerParams(dimension_semantics=sem, vmem_limit_bytes=VMEM_LIMIT)


def _const_spec(shape):
    nd = len(shape)
    return pl.BlockSpec(shape, lambda *_: (0,) * nd, pipeline_mode=pl.Buffered(1))


def _rms(xf, g):
    y = xf * lax.rsqrt(jnp.mean(xf * xf, axis=-1, keepdims=True) + EPS)
    return y * g


def _split_bf16(v):
    hi = v.astype(BF16)
    lo = (v - hi.astype(F32)).astype(BF16)
    return hi, lo


def _part_starts(parts, tm):
    starts = [0]
    for a in parts:
        assert a.shape[0] % tm == 0
        starts.append(starts[-1] + a.shape[0] // tm)
    return tuple(starts)


def _part_index(i, *_, lo, nt):
    return (jnp.clip(i - lo, 0, nt - 1), 0)


def _part_specs(parts, tm, width):
    starts = _part_starts(parts, tm)
    return [pl.BlockSpec((tm, width), functools.partial(_part_index, lo=lo, nt=hi - lo))
            for lo, hi in zip(starts[:-1], starts[1:])]


def _read_parts(i, refs, starts):
    v = refs[0][...]
    for r, lo in zip(refs[1:], starts[1:]):
        v = jnp.where(i >= lo, r[...], v)
    return v


def _write_parts(i, refs, starts, v):
    if len(refs) == 1:
        refs[0][...] = v
        return
    for r, lo, hi in zip(refs, starts[:-1], starts[1:]):
        @pl.when(jnp.logical_and(i >= lo, i < hi))
        def _(r=r):
            r[...] = v


def _proj_kernel(pos_ref, *refs, x_starts):
    del pos_ref
    n_x = len(x_starts) - 1
    x_refs = refs[:n_x]
    (gmix_ref, win_ref, gsum_ref, gqna_ref, gkna_ref, gqa_ref, wuq_ref, gkva_ref, wukv_ref, gq_ref, gk_ref,
     cos_ref, sin_ref, qna_ref, kna_ref, vna_ref, qm_ref, km_ref, vm_ref) = refs[n_x:]
    h = _rms(_read_parts(pl.program_id(0), x_refs, x_starts), gmix_ref[...]).astype(BF16)

    def proj(lo, hi):
        return jnp.dot(h, win_ref[:, lo:hi], preferred_element_type=F32)

    gsum = gsum_ref[...]

    def head_norm(v, gain):
        v2 = (v * v).astype(BF16)
        w = gsum.shape[0]
        ss = jnp.concatenate([jnp.dot(v2[:, c:c + w], gsum, preferred_element_type=F32)
                              for c in range(0, NA_WIDTH, w)], axis=1)
        return (v * lax.rsqrt(ss * (1.0 / NA_HEAD_DIM) + EPS)) * gain

    o1, o2, o3 = NA_WIDTH, 2 * NA_WIDTH, 3 * NA_WIDTH
    o4 = o3 + MLA_Q_LORA
    o5 = o4 + MLA_KV_LORA
    cos = cos_ref[...]
    sin = sin_ref[...]

    def rope(u):
        return u * cos + pltpu.roll(u, 64, 1) * sin

    cq = _rms(proj(o3, o4), gqa_ref[...]).astype(BF16)
    q_all = jnp.dot(cq, wuq_ref[...], preferred_element_type=F32)
    gq = gq_ref[...]
    for hd in range(MLA_HEADS):
        qh = q_all[:, hd * MLA_QK_PAD:(hd + 1) * MLA_QK_PAD]
        r = lax.rsqrt(jnp.sum(qh * qh, axis=-1, keepdims=True) * (1.0 / MLA_QK) + EPS)
        qn = (qh * r) * gq
        qm_ref[hd, 0:MLA_NOPE, :] = qn[:, 0:MLA_NOPE].T.astype(BF16)
        qm_ref[hd, MLA_NOPE:MLA_QK_PAD, :] = rope(qn[:, MLA_NOPE:MLA_QK_PAD]).T.astype(BF16)

    ckv = _rms(proj(o4, o5), gkva_ref[...]).astype(BF16)
    kv = jnp.dot(ckv, wukv_ref[...], preferred_element_type=F32)
    kpe = proj(o5, o5 + LANES)
    ss_pe = jnp.sum(kpe * kpe, axis=-1, keepdims=True)
    gk = gk_ref[...]
    for hd in range(MLA_HEADS):
        base = hd * (MLA_NOPE + MLA_V)
        kn = kv[:, base:base + MLA_NOPE]
        r = lax.rsqrt((jnp.sum(kn * kn, axis=-1, keepdims=True) + ss_pe) * (1.0 / MLA_QK) + EPS)
        km_ref[hd, :, 0:MLA_NOPE] = ((kn * r) * gk[:, 0:MLA_NOPE]).astype(BF16)
        km_ref[hd, :, MLA_NOPE:MLA_QK_PAD] = rope((kpe * r) * gk[:, MLA_NOPE:MLA_QK_PAD]).astype(BF16)
        vm_ref[hd, 0] = kv[:, base + MLA_NOPE:base + MLA_NOPE + MLA_V].T.astype(BF16)

    qna_ref[...] = head_norm(proj(0, o1), gqna_ref[...]).astype(BF16)
    kna_ref[...] = head_norm(proj(o1, o2), gkna_ref[...]).astype(BF16)
    vna_ref[...] = proj(o2, o3).astype(BF16)


def _proj(x_parts, pos_blk, lw, cos_t, sin_t):
    tm = TOKEN_TILE
    x_starts = _part_starts(x_parts, tm)
    n = x_starts[-1] * tm
    row = lambda i, pos: (i, 0)
    head = lambda i, pos: (0, i, 0)
    in_specs = _part_specs(x_parts, tm, D_MODEL) + [
        _const_spec((1, D_MODEL)),
        _const_spec(lw["w_in"].shape),
        _const_spec((MXU_WIDTH, MXU_WIDTH)),
        _const_spec((1, NA_WIDTH)),
        _const_spec((1, NA_WIDTH)),
        _const_spec((1, MLA_Q_LORA)),
        _const_spec(lw["w_uq"].shape),
        _const_spec((1, MLA_KV_LORA)),
        _const_spec(lw["w_ukv"].shape),
        _const_spec((1, MLA_QK_PAD)),
        _const_spec((1, MLA_QK_PAD)),
        pl.BlockSpec((tm, LANES), lambda i, pos: (pos[i], 0)),
        pl.BlockSpec((tm, LANES), lambda i, pos: (pos[i], 0)),
    ]
    out_specs = [
        pl.BlockSpec((tm, NA_WIDTH), row),
        pl.BlockSpec((tm, NA_WIDTH), row),
        pl.BlockSpec((tm, NA_WIDTH), row),
        pl.BlockSpec((MLA_HEADS, MLA_QK_PAD, tm), lambda i, pos: (0, 0, i)),
        pl.BlockSpec((MLA_HEADS, tm, MLA_QK_PAD), head),
        pl.BlockSpec((MLA_HEADS, 1, MLA_V, tm), lambda i, pos: (0, i, 0, 0)),
    ]
    out_shape = [
        jax.ShapeDtypeStruct((n, NA_WIDTH), BF16),
        jax.ShapeDtypeStruct((n, NA_WIDTH), BF16),
        jax.ShapeDtypeStruct((n, NA_WIDTH), BF16),
        jax.ShapeDtypeStruct((MLA_HEADS, MLA_QK_PAD, n), BF16),
        jax.ShapeDtypeStruct((MLA_HEADS, n, MLA_QK_PAD), BF16),
        jax.ShapeDtypeStruct((MLA_HEADS, n // tm, MLA_V, tm), BF16),
    ]
    return pl.pallas_call(
        functools.partial(_proj_kernel, x_starts=x_starts),
        grid_spec=pltpu.PrefetchScalarGridSpec(
            num_scalar_prefetch=1, grid=(n // tm,), in_specs=in_specs, out_specs=out_specs),
        out_shape=out_shape,
        compiler_params=_cparams(("arbitrary",)),
        name="proj",
    )(pos_blk, *x_parts, lw["g_mix"], lw["w_in"], lw["gsum"], lw["g_qna"], lw["g_kna"], lw["g_qa"], lw["w_uq"],
      lw["g_kva"], lw["w_ukv"], lw["g_q"], lw["g_k"], cos_t, sin_t)


def _na_kernel(w0_ref, kind_ref, q_ref, k_ref, v_ref, bias_ref, o_ref):
    del w0_ref
    kind = kind_ref[pl.program_id(0)]
    is_first = kind == 0
    is_last = kind == 2
    lane = lax.broadcasted_iota(jnp.int32, (GRID_W, LANES), 1)
    lo_half = lane < NA_HEAD_DIM
    half_rows = NA_ROWS // 2

    def one_row(j, carry):
        sh = j - half_rows
        off = jnp.where(is_first, jnp.maximum(sh, 0), jnp.where(is_last, NA_BLOCK_ROWS + jnp.minimum(sh, 0), j))
        dcls = jnp.where(is_first, jnp.minimum(j, half_rows), jnp.where(is_last, jnp.maximum(j, half_rows), half_rows))
        kstart = pl.multiple_of(off * GRID_W, GRID_W)
        qstart = pl.multiple_of(j * GRID_W, GRID_W)
        scores = []
        for p in range(NA_HEADS // 2):
            cols = slice(p * LANES, (p + 1) * LANES)
            qp = q_ref[pl.ds(qstart, GRID_W), cols]
            kp = k_ref[pl.ds(kstart, NA_ROWS * GRID_W), cols]
            zero = jnp.zeros_like(qp)
            q2 = jnp.concatenate([jnp.where(lo_half, qp, zero), jnp.where(lo_half, zero, qp)], axis=0)
            s = lax.dot_general(q2, kp, (((1,), (1,)), ((), ())), preferred_element_type=F32)
            scores.append(s + bias_ref[dcls, p])
        s = jnp.concatenate(scores, axis=0)
        e = jnp.exp2(s - jnp.max(s, axis=-1, keepdims=True))
        inv_l = 1.0 / jnp.sum(e, axis=-1, keepdims=True)
        pb = e.astype(BF16)
        for p in range(NA_HEADS // 2):
            cols = slice(p * LANES, (p + 1) * LANES)
            rows = slice(2 * p * GRID_W, (2 * p + 2) * GRID_W)
            vp = v_ref[pl.ds(kstart, NA_ROWS * GRID_W), cols]
            o2 = jnp.dot(pb[rows], vp, preferred_element_type=F32) * inv_l[rows]
            o_ref[pl.ds(qstart, GRID_W), cols] = jnp.where(lo_half, o2[0:GRID_W], o2[GRID_W:2 * GRID_W])
        return carry

    lax.fori_loop(0, NA_BLOCK_ROWS, one_row, 0, unroll=NA_ROW_UNROLL)


def _na_bias_table(rpb):
    c = np.arange(GRID_W)
    cs = np.clip(c - NA_COLS // 2, 0, GRID_W - NA_COLS)
    kc = np.arange(GRID_W)
    valid = (kc[None, :] >= cs[:, None]) & (kc[None, :] < cs[:, None] + NA_COLS)
    dc = kc[None, :] - c[:, None] + (NA_COLS - 1)
    onehot = ((np.arange(2 * NA_COLS - 1)[:, None, None] == dc[None]) & valid[None]).astype(np.float32)
    t = jnp.einsum("hrd,dck->hrck", rpb.astype(F32), jnp.asarray(onehot), precision=lax.Precision.HIGHEST)
    t = jnp.where(jnp.asarray(valid)[None, None], t * LOG2E, NEG_INF)
    tab = jnp.stack([t[:, NA_ROWS - 1 - d:2 * NA_ROWS - 1 - d] for d in range(NA_ROWS)], axis=0)
    tab = jnp.transpose(tab, (0, 1, 3, 2, 4))
    return tab.reshape(NA_ROWS, NA_HEADS // 2, 2 * GRID_W, NA_ROWS * GRID_W)


def _na_block_tables(segments):
    w0, kind = [], []
    tok = 0
    for b, s in segments:
        rows = s // GRID_W
        nb = rows // NA_BLOCK_ROWS
        for _ in range(b):
            for i in range(nb):
                r0 = min(max(NA_BLOCK_ROWS * i - NA_ROWS // 2, 0), rows - NA_WINDOW_ROWS)
                w0.append(tok // GRID_W + r0)
                kind.append(0 if i == 0 else (2 if i == nb - 1 else 1))
            tok += s
    return jnp.asarray(w0, jnp.int32), jnp.asarray(kind, jnp.int32)


def _na(q, k, v, bias, segments):
    n = q.shape[0]
    tq = NA_BLOCK_ROWS * GRID_W
    tw = NA_WINDOW_ROWS * GRID_W
    w0, kind = _na_block_tables(segments)
    win = pl.BlockSpec((pl.Element(tw), pl.Element(NA_WIDTH)), lambda i, w0, kind: (w0[i] * GRID_W, 0))
    return pl.pallas_call(
        _na_kernel,
        grid_spec=pltpu.PrefetchScalarGridSpec(
            num_scalar_prefetch=2,
            grid=(n // tq,),
            in_specs=[
                pl.BlockSpec((tq, NA_WIDTH), lambda i, w0, kind: (i, 0)),
                win,
                win,
                pl.BlockSpec(bias.shape, lambda i, w0, kind: (0, 0, 0, 0), pipeline_mode=pl.Buffered(1)),
            ],
            out_specs=pl.BlockSpec((tq, NA_WIDTH), lambda i, w0, kind: (i, 0)),
        ),
        out_shape=jax.ShapeDtypeStruct((n, NA_WIDTH), F32),
        compiler_params=_cparams(("arbitrary",)),
        name="na",
    )(w0, kind, q, k, v, bias)


def _flash_kernel(qt_ref, qt_next_ref, k_ref, vt_ref, o_ref, s_scr, acc_ref, *, n_chunks, tk, per_trip):
    acc_ref[...] = jnp.zeros(acc_ref.shape, F32)
    qt = qt_ref[...]
    tq = qt.shape[1]

    def scores(c, slot, q=None):
        start = pl.multiple_of(c * tk, tk)
        s_scr[slot] = jnp.dot(k_ref[pl.ds(start, tk), :], qt if q is None else q,
                              preferred_element_type=F32)

    def softmax_pv(c, slot, m_prev, l_prev):
        st = s_scr[slot]
        m_new = jnp.maximum(m_prev, jnp.max(st, axis=0, keepdims=True))
        alpha = jnp.exp2(m_prev - m_new)
        pt = jnp.exp2(st - m_new)
        l_new = alpha * l_prev + jnp.sum(pt, axis=0, keepdims=True)
        acc_ref[...] = alpha * acc_ref[...] + jnp.dot(vt_ref[c], pt.astype(BF16), preferred_element_type=F32)
        return m_new, l_new

    @pl.when(pl.program_id(2) == 0)
    def _():
        scores(0, 0)

    def trip(base, m, l, final):
        for u in range(per_trip // 2):
            c0 = base + 2 * u
            scores(c0 + 1, 1)
            m, l = softmax_pv(c0, 0, m, l)
            if final and u == per_trip // 2 - 1:
                scores(0, 0, qt_next_ref[...])
            else:
                scores(c0 + 2, 0)
            m, l = softmax_pv(c0 + 1, 1, m, l)
        return m, l

    n_trips = n_chunks // per_trip
    init = (jnp.full((1, tq), -jnp.inf, F32), jnp.zeros((1, tq), F32))
    m, l = lax.fori_loop(0, n_trips - 1, lambda t, c: trip(per_trip * t, c[0], c[1], False), init)
    _, l = trip(per_trip * (n_trips - 1), m, l, True)
    o_ref[...] = (acc_ref[...] / l).T


def _flash(qt, k, vt, tok_off, b, s):
    n = k.shape[1]
    tq, tk = FLASH_TQ, FLASH_TK
    assert tok_off % s == 0 and s % tq == 0 and s % (2 * tk) == 0 and vt.shape[-1] == tk
    nq = s // tq
    nc = s // tk
    per_trip = math.gcd(nc, FLASH_CHUNKS_PER_TRIP)
    qblk0 = tok_off // tq
    sblk0 = tok_off // s
    return pl.pallas_call(
        functools.partial(_flash_kernel, n_chunks=nc, tk=tk, per_trip=per_trip),
        grid=(b, MLA_HEADS, nq),
        in_specs=[
            pl.BlockSpec((None, MLA_QK_PAD, tq), lambda bi, h, i: (h, 0, qblk0 + bi * nq + i)),
            pl.BlockSpec((None, MLA_QK_PAD, tq),
                         lambda bi, h, i: (h, 0, qblk0 + bi * nq + jnp.minimum(i + 1, nq - 1))),
            pl.BlockSpec((None, s, MLA_QK_PAD), lambda bi, h, i: (h, sblk0 + bi, 0)),
            pl.BlockSpec((None, nc, MLA_V, tk), lambda bi, h, i: (h, sblk0 + bi, 0, 0)),
        ],
        out_specs=pl.BlockSpec((tq, MLA_V), lambda bi, h, i: (bi * nq + i, h)),
        out_shape=jax.ShapeDtypeStruct((b * s, MLA_WIDTH), F32),
        scratch_shapes=[pltpu.VMEM((2, tk, tq), F32), pltpu.VMEM((MLA_V, tq), F32)],
        compiler_params=_cparams(("arbitrary", "arbitrary", "arbitrary")),
        name="flash",
    )(qt, qt, k, vt)


def _swiglu(h, wg_ref, wu_ref, wd_ref, acc=None):
    d_ff = wg_ref.shape[-1]
    assert d_ff % FFN_CHUNK == 0
    for c in range(d_ff // FFN_CHUNK):
        sl = slice(c * FFN_CHUNK, (c + 1) * FFN_CHUNK)
        g = jnp.dot(h, wg_ref[:, sl], preferred_element_type=F32)
        u = jnp.dot(h, wu_ref[:, sl], preferred_element_type=F32)
        act = ((g * jax.nn.sigmoid(g)) * u).astype(BF16)
        y = jnp.dot(act, wd_ref[sl, :], preferred_element_type=F32)
        acc = y if acc is None else acc + y
    return acc


def _merge_kernel(*refs, route, x_starts, b_starts):
    n_x, n_b = len(x_starts) - 1, len(b_starts) - 1
    x_refs, a_ref, b_refs = refs[:n_x], refs[n_x], refs[n_x + 1:n_x + 1 + n_b]
    rest = refs[n_x + 1 + n_b:]
    if route:
        ga_ref, gb_ref, wout_ref, gffn_ref, wr_ref, xo_ref, h_ref, idx_ref, gate_ref = rest
    else:
        ga_ref, gb_ref, wout_ref, gffn_ref, wg_ref, wu_ref, wd_ref, xo_ref = rest
    i = pl.program_id(0)
    a = _rms(a_ref[...], ga_ref[...]).astype(BF16)
    b = _rms(_read_parts(i, b_refs, b_starts), gb_ref[...]).astype(BF16)
    y = (jnp.dot(a, wout_ref[0:NA_WIDTH, :], preferred_element_type=F32)
         + jnp.dot(b, wout_ref[NA_WIDTH:NA_WIDTH + MLA_WIDTH, :], preferred_element_type=F32))
    xn = _read_parts(i, x_refs, x_starts) + y
    hf = _rms(xn, gffn_ref[...])
    if not route:
        xo_ref[...] = _swiglu(hf.astype(BF16), wg_ref, wu_ref, wd_ref, acc=xn)
        return
    xo_ref[...] = xn
    h_ref[...] = hf
    hh, hl = _split_bf16(hf)
    r1 = jnp.dot(hh, wr_ref[...], preferred_element_type=F32)
    r2 = jnp.dot(hl, wr_ref[:, 0:LANES], preferred_element_type=F32)
    logits = r1[:, 0:LANES] + (r1[:, LANES:2 * LANES] + r2)
    lg = logits.T[0:N_EXPERTS, :]
    eid = lax.broadcasted_iota(jnp.int32, lg.shape, 0)
    t1 = jnp.max(lg, axis=0, keepdims=True)
    i1 = jnp.min(jnp.where(lg == t1, eid, N_EXPERTS), axis=0, keepdims=True)
    lg2 = jnp.where(eid == i1, -jnp.inf, lg)
    t2 = jnp.max(lg2, axis=0, keepdims=True)
    i2 = jnp.min(jnp.where(lg2 == t2, eid, N_EXPERTS), axis=0, keepdims=True)
    e2 = jnp.exp(t2 - t1)
    den = 1.0 + e2
    idx_ref[...] = jnp.where(eid == 0, i1, jnp.where(eid == 1, i2, 0))
    gate_ref[...] = jnp.where(eid == 0, 1.0 / den, jnp.where(eid == 1, e2 / den, 0.0))


def _merge(x_parts, out_a, b_parts, lw, ffn_weights=None):
    route = ffn_weights is None
    tm = TOKEN_TILE
    x_starts = _part_starts(x_parts, tm)
    b_starts = _part_starts(b_parts, tm)
    n = out_a.shape[0]
    assert x_starts[-1] * tm == n and b_starts[-1] * tm == n
    row = lambda i: (i, 0)
    in_specs = _part_specs(x_parts, tm, D_MODEL) + [pl.BlockSpec((tm, NA_WIDTH), row)] + _part_specs(
        b_parts, tm, MLA_WIDTH) + [
        _const_spec((1, NA_WIDTH)),
        _const_spec((1, MLA_WIDTH)),
        _const_spec((NA_WIDTH + MLA_WIDTH, D_MODEL)),
        _const_spec((1, D_MODEL)),
    ]
    args = [*x_parts, out_a, *b_parts, lw["g_oa"], lw["g_ob"], lw["w_out"], lw["g_ffn"]]
    out_specs = [pl.BlockSpec((tm, D_MODEL), row)]
    out_shape = [jax.ShapeDtypeStruct((n, D_MODEL), F32)]
    if route:
        in_specs += [_const_spec((D_MODEL, 2 * LANES))]
        args += [lw["w_router"]]
        col = lambda i: (0, i)
        out_specs += [pl.BlockSpec((tm, D_MODEL), row), pl.BlockSpec((N_EXPERTS, tm), col),
                      pl.BlockSpec((N_EXPERTS, tm), col)]
        out_shape += [jax.ShapeDtypeStruct((n, D_MODEL), F32), jax.ShapeDtypeStruct((N_EXPERTS, n), jnp.int32),
                      jax.ShapeDtypeStruct((N_EXPERTS, n), F32)]
    else:
        in_specs += [_const_spec(w.shape) for w in ffn_weights]
        args += list(ffn_weights)
    out = pl.pallas_call(
        functools.partial(_merge_kernel, route=route, x_starts=x_starts, b_starts=b_starts),
        grid=(n // tm,),
        in_specs=in_specs,
        out_specs=out_specs,
        out_shape=out_shape,
        compiler_params=_cparams(("arbitrary",)),
        name="merge_route" if route else "merge_ffn",
    )(*args)
    return out if route else out[0]


def _ffn_kernel(te_ref, src_ref, nv_ref, h_ref, wg_ref, wu_ref, wd_ref, o_ref):
    del te_ref, src_ref

    @pl.when(pl.program_id(0) < nv_ref[0])
    def _():
        o_ref[...] = _swiglu(h_ref[...].astype(BF16), wg_ref, wu_ref, wd_ref)

    @pl.when(pl.program_id(0) >= nv_ref[0])
    def _():
        o_ref[...] = jnp.zeros(o_ref.shape, o_ref.dtype)


def _ffn(h, wg, wu, wd, tile_expert, tile_src, n_valid):
    n = h.shape[0]
    tm = TOKEN_TILE
    d_ff = wg.shape[-1]
    wspec = lambda shp: pl.BlockSpec((None,) + shp, lambda i, te, src, nv: (te[i], 0, 0),
                                     pipeline_mode=pl.Buffered(1))
    return pl.pallas_call(
        _ffn_kernel,
        grid_spec=pltpu.PrefetchScalarGridSpec(
            num_scalar_prefetch=3, grid=(n // tm,),
            in_specs=[pl.BlockSpec((tm, D_MODEL), lambda i, te, src, nv: (src[i], 0)),
                      wspec((D_MODEL, d_ff)), wspec((D_MODEL, d_ff)), wspec((d_ff, D_MODEL))],
            out_specs=pl.BlockSpec((tm, D_MODEL), lambda i, te, src, nv: (i, 0))),
        out_shape=jax.ShapeDtypeStruct((n, D_MODEL), F32),
        compiler_params=_cparams(("arbitrary",)),
        name="expert_ffn",
    )(tile_expert, tile_src, n_valid, h, wg, wu, wd)


def _row_copy(src_ref, src_row, dst_ref, dst_row, sem):
    return pltpu.make_async_copy(src_ref.at[pl.ds(src_row, 1), :], dst_ref.at[pl.ds(dst_row, 1), :], sem)


def _dispatch_kernel(meta_ref, pos_ref, h_ref, xs_ref, zero_ref, stage_ref, sem, zsem):
    tm = h_ref.shape[0]
    n_tiles = xs_ref.shape[0] // tm

    @pl.when(pl.program_id(0) == 0)
    def _():
        zero_ref[...] = jnp.zeros(zero_ref.shape, zero_ref.dtype)

        def tile_copy(t):
            return pltpu.make_async_copy(zero_ref, xs_ref.at[pl.ds(pl.multiple_of(t * tm, tm), tm), :], zsem)

        def each(fn):
            lax.fori_loop(meta_ref[2 * N_EXPERTS], n_tiles, lambda t, c: (fn(tile_copy(t)), c)[1], 0)
            for e in range(N_EXPERTS):
                lax.fori_loop(meta_ref[e], meta_ref[N_EXPERTS + e],
                              lambda r, c: (fn(_row_copy(zero_ref, 0, xs_ref, r, zsem)), c)[1], 0)

        each(lambda cp: cp.start())
        each(lambda cp: cp.wait())

    i = pl.program_id(0)
    n_steps = pl.num_programs(0)
    slot = lax.rem(i, 2)

    def wait_slot(s):
        for _ in range(TOP_K):
            pltpu.make_async_copy(stage_ref.at[s], xs_ref.at[pl.ds(0, tm), :], sem.at[s]).wait()

    @pl.when(i >= 2)
    def _():
        wait_slot(slot)

    stage_ref[slot] = h_ref[...]

    def issue(r, c):
        for kk in range(TOP_K):
            _row_copy(stage_ref.at[slot], r, xs_ref, pos_ref[TOP_K * r + kk], sem.at[slot]).start(priority=kk % 2)
        return c

    lax.fori_loop(0, tm, issue, 0, unroll=ROW_DMA_UNROLL)

    @pl.when(i == n_steps - 1)
    def _():
        wait_slot(slot)

        @pl.when(n_steps > 1)
        def _():
            wait_slot(1 - slot)


def _dispatch(h, pos_flat, n_slots, meta):
    n = h.shape[0]
    tm = TOKEN_TILE
    return pl.pallas_call(
        _dispatch_kernel,
        grid_spec=pltpu.PrefetchScalarGridSpec(
            num_scalar_prefetch=1,
            grid=(n // tm,),
            in_specs=[
                pl.BlockSpec((TOP_K * tm,), lambda i, meta: (i,), memory_space=pltpu.SMEM),
                pl.BlockSpec((tm, D_MODEL), lambda i, meta: (i, 0)),
            ],
            out_specs=pl.BlockSpec(memory_space=pl.ANY),
            scratch_shapes=[pltpu.VMEM((tm, D_MODEL), F32), pltpu.VMEM((2, tm, D_MODEL), F32),
                            pltpu.SemaphoreType.DMA((2,)), pltpu.SemaphoreType.DMA(())],
        ),
        out_shape=jax.ShapeDtypeStruct((n_slots, D_MODEL), F32),
        compiler_params=_cparams(("arbitrary",)),
        name="dispatch",
    )(meta, pos_flat, h)


def _combine_kernel(pos_ref, pos_next_ref, x_ref, gate_ref, ys_ref, *rest, out_starts):
    n_out = len(out_starts) - 1
    o_refs, (buf_ref, sem) = rest[:n_out], rest[n_out:]
    tm = x_ref.shape[0]
    i = pl.program_id(0)
    slot = lax.rem(i, 2)

    def gather(p_ref, s):
        def issue(r, c):
            for kk in range(TOP_K):
                _row_copy(ys_ref, p_ref[TOP_K * r + kk], buf_ref.at[s, kk], r, sem.at[s]).start(priority=kk % 2)
            return c

        lax.fori_loop(0, tm, issue, 0, unroll=ROW_DMA_UNROLL)

    @pl.when(i == 0)
    def _():
        gather(pos_ref, 0)

    @pl.when(i + 1 < pl.num_programs(0))
    def _():
        gather(pos_next_ref, 1 - slot)

    for kk in range(TOP_K):
        pltpu.make_async_copy(ys_ref.at[pl.ds(0, tm), :], buf_ref.at[slot, kk], sem.at[slot]).wait()
    gate = gate_ref[...].T
    out = gate[:, 0:1] * buf_ref[slot, 0] + gate[:, 1:2] * buf_ref[slot, 1]
    _write_parts(i, o_refs, out_starts, x_ref[...] + out)


def _combine(x, gate_t, ys, pos_flat, out_rows):
    n = x.shape[0]
    tm = TOKEN_TILE
    n_t = n // tm
    outs = [jax.ShapeDtypeStruct((r, D_MODEL), F32) for r in out_rows]
    out_starts = _part_starts(outs, tm)
    assert out_starts[-1] == n_t
    return pl.pallas_call(
        functools.partial(_combine_kernel, out_starts=out_starts),
        grid=(n_t,),
        in_specs=[
            pl.BlockSpec((TOP_K * tm,), lambda i: (i,), memory_space=pltpu.SMEM),
            pl.BlockSpec((TOP_K * tm,), lambda i: (jnp.minimum(i + 1, n_t - 1),), memory_space=pltpu.SMEM),
            pl.BlockSpec((tm, D_MODEL), lambda i: (i, 0)),
            pl.BlockSpec((N_EXPERTS, tm), lambda i: (0, i)),
            pl.BlockSpec(memory_space=pl.ANY),
        ],
        out_specs=_part_specs(outs, tm, D_MODEL),
        out_shape=outs,
        scratch_shapes=[pltpu.VMEM((2, TOP_K, tm, D_MODEL), F32), pltpu.SemaphoreType.DMA((2,))],
        compiler_params=_cparams(("arbitrary",)),
        name="combine",
    )(pos_flat, pos_flat, x, gate_t, ys)


def _moe(x, h, idx, gate, wg, wu, wd, out_rows):
    n = x.shape[0]
    tm = TOKEN_TILE
    n_tiles = TOP_K * n // tm + N_EXPERTS
    n_slots = n_tiles * tm
    first = idx[0] < idx[1]
    e_lo = jnp.where(first, idx[0], idx[1])
    e_hi = jnp.where(first, idx[1], idx[0])
    g_lo = jnp.where(first, gate[0], gate[1])
    g_hi = jnp.where(first, gate[1], gate[0])
    experts = jnp.arange(N_EXPERTS, dtype=jnp.int32)
    sel = (e_lo[None, :] == experts[:, None]) | (e_hi[None, :] == experts[:, None])
    sel_b = sel.reshape(N_EXPERTS, n // tm, tm).astype(BF16)
    tri = (jnp.arange(tm)[:, None] <= jnp.arange(tm)[None, :]).astype(BF16)
    within = jnp.einsum("ebj,jt->ebt", sel_b, tri, preferred_element_type=F32).astype(jnp.int32)
    block_tot = within[:, :, -1]
    block_off = jnp.cumsum(block_tot, axis=1) - block_tot
    csum = (within + block_off[:, :, None]).reshape(N_EXPERTS, n)
    counts = csum[:, -1]
    padded = ((counts + tm - 1) // tm) * tm
    ends = jnp.cumsum(padded)
    starts = ends - padded
    slot = starts[:, None] + csum - 1
    pos_lo = jnp.sum(jnp.where(e_lo[None, :] == experts[:, None], slot, 0), axis=0)
    pos_hi = jnp.sum(jnp.where(e_hi[None, :] == experts[:, None], slot, 0), axis=0)
    pos_flat = jnp.stack([pos_lo, pos_hi], axis=1).reshape(-1).astype(jnp.int32)
    n_valid = (ends[-1] // tm).astype(jnp.int32)
    tile_src = jnp.minimum(jnp.arange(n_tiles, dtype=jnp.int32), n_valid - 1)
    tile_expert = jnp.minimum(
        jnp.sum((tile_src[:, None] * tm >= ends[None, :]).astype(jnp.int32), axis=1), N_EXPERTS - 1).astype(jnp.int32)
    gate_t = jnp.concatenate([g_lo[None], g_hi[None], jnp.zeros((N_EXPERTS - TOP_K, n), F32)], axis=0)
    meta = jnp.concatenate([starts + counts, ends, n_valid.reshape(1)]).astype(jnp.int32)
    xs = _dispatch(h, pos_flat, n_slots, meta)
    ys = _ffn(xs, wg, wu, wd, tile_expert, tile_src, n_valid.reshape(1))
    return _combine(x, gate_t, ys, pos_flat, out_rows)


def _rope_layout(w64):
    z = jnp.zeros(w64.shape[:-1] + (MLA_ROPE // 2,), w64.dtype)
    return jnp.concatenate([w64[..., :MLA_ROPE // 2], z, w64[..., MLA_ROPE // 2:], z], axis=-1)


def _qk_layout(w192):
    return jnp.concatenate([w192[..., :MLA_NOPE], _rope_layout(w192[..., MLA_NOPE:])], axis=-1)


def _layer_weights(l, norm_mix, norm_ffn, w_in, na_q_norm, na_k_norm, mla_q_a_norm, w_uq, mla_kv_a_norm, w_ukv,
                   mla_q_norm, mla_k_norm, na_out_norm, mla_out_norm, w_out):
    o5 = 3 * NA_WIDTH + MLA_Q_LORA + MLA_KV_LORA
    w_in_l = w_in[l]
    w_in_p = jnp.concatenate([w_in_l[:, :o5], _rope_layout(w_in_l[:, o5:])], axis=1)
    w_uq_p = _qk_layout(w_uq[l].reshape(MLA_Q_LORA, MLA_HEADS, MLA_QK)).reshape(MLA_Q_LORA, MLA_HEADS * MLA_QK_PAD)
    grp = jnp.arange(MXU_WIDTH) // NA_HEAD_DIM
    row = lambda v: v.reshape(1, -1).astype(F32)
    return {
        "g_mix": row(norm_mix[l]),
        "g_ffn": row(norm_ffn[l]),
        "w_in": w_in_p.astype(BF16),
        "gsum": (grp[:, None] == grp[None, :]).astype(BF16),
        "g_qna": row(jnp.tile(na_q_norm[l], NA_HEADS) * (NA_HEAD_DIM ** -0.5 * LOG2E)),
        "g_kna": row(jnp.tile(na_k_norm[l], NA_HEADS)),
        "g_qa": row(mla_q_a_norm[l]),
        "w_uq": w_uq_p.astype(BF16),
        "g_kva": row(mla_kv_a_norm[l]),
        "w_ukv": w_ukv[l].astype(BF16),
        "g_q": row(_qk_layout(mla_q_norm[l]) * (MLA_QK ** -0.5 * LOG2E)),
        "g_k": row(_qk_layout(mla_k_norm[l])),
        "g_oa": row(na_out_norm[l]),
        "g_ob": row(mla_out_norm[l]),
        "w_out": w_out[l].astype(BF16),
    }


def _rope_tables(s_max):
    inv = ROPE_THETA ** (-jnp.arange(0, MLA_ROPE, 2, dtype=F32) / MLA_ROPE)
    ang = jnp.arange(s_max, dtype=F32)[:, None] * inv[None, :]
    cos, sin = jnp.cos(ang), jnp.sin(ang)
    z = jnp.zeros_like(cos)
    return jnp.concatenate([cos, z, cos, z], axis=1), jnp.concatenate([-sin, z, sin, z], axis=1)


def _trunk(x_parts, segments, norm_mix, norm_ffn, w_in, na_q_norm, na_k_norm, na_rpb, mla_q_a_norm, w_uq,
           mla_kv_a_norm, w_ukv, mla_q_norm, mla_k_norm, na_out_norm, mla_out_norm, w_out, ffn_w_gate, ffn_w_up,
           ffn_w_down, moe_router, moe_w_gate, moe_w_up, moe_w_down):
    x_parts = tuple(x_parts)
    part_rows = tuple(a.shape[0] for a in x_parts)
    assert part_rows == tuple(b * s for b, s in segments)
    n = sum(part_rows)
    tm = TOKEN_TILE
    depth = w_in.shape[0]
    s_max = max(s for _, s in segments)
    cos_t, sin_t = _rope_tables(s_max)
    pos_blk = jnp.asarray([i for b, s in segments for _ in range(b) for i in range(s // tm)], jnp.int32)
    for l in range(depth):
        lw = _layer_weights(l, norm_mix, norm_ffn, w_in, na_q_norm, na_k_norm, mla_q_a_norm, w_uq, mla_kv_a_norm,
                            w_ukv, mla_q_norm, mla_k_norm, na_out_norm, mla_out_norm, w_out)
        qna, kna, vna, qm, km, vm = _proj(x_parts, pos_blk, lw, cos_t, sin_t)
        out_a = _na(qna, kna, vna, _na_bias_table(na_rpb[l]), segments)
        b_parts = []
        tok = 0
        for b, s in segments:
            b_parts.append(_flash(qm, km, vm, tok, b, s))
            tok += b * s
        i = l // 2
        if l % 2 == 0:
            x = _merge(x_parts, out_a, b_parts, lw,
                       (ffn_w_gate[i].astype(BF16), ffn_w_up[i].astype(BF16), ffn_w_down[i].astype(BF16)))
        else:
            wr = jnp.pad(moe_router[i], ((0, 0), (0, LANES - N_EXPERTS)))
            lw["w_router"] = jnp.concatenate(_split_bf16(wr), axis=1)
            x, h, idx, gate = _merge(x_parts, out_a, b_parts, lw)
            out_rows = part_rows if l == depth - 1 else (n,)
            x_parts = _moe(x, h, idx[:TOP_K], gate[:TOP_K], moe_w_gate[i].astype(BF16), moe_w_up[i].astype(BF16),
                           moe_w_down[i].astype(BF16), out_rows)
            continue
        x_parts = (x,)
    if len(x_parts) == len(part_rows):
        return tuple(x_parts)
    (x,) = x_parts
    splits = np.cumsum((0,) + part_rows)
    return tuple(x[a:b] for a, b in zip(splits[:-1], splits[1:]))


def kernel(x_prompt, x_sample, norm_mix, norm_ffn, w_in, na_q_norm, na_k_norm, na_rpb, mla_q_a_norm, w_uq,
           mla_kv_a_norm, w_ukv, mla_q_norm, mla_k_norm, na_out_norm, mla_out_norm, w_out, ffn_w_gate, ffn_w_up,
           ffn_w_down, moe_router, moe_w_gate, moe_w_up, moe_w_down):
    bp, sp, d = x_prompt.shape
    bs, ss, _ = x_sample.shape
    segments = ((bp, sp), (bs, ss))
    y_p, y_s = _trunk((x_prompt.reshape(bp * sp, d), x_sample.reshape(bs * ss, d)), segments, norm_mix, norm_ffn,
                      w_in, na_q_norm, na_k_norm, na_rpb, mla_q_a_norm, w_uq, mla_kv_a_norm, w_ukv, mla_q_norm,
                      mla_k_norm, na_out_norm, mla_out_norm, w_out, ffn_w_gate, ffn_w_up, ffn_w_down, moe_router,
                      moe_w_gate, moe_w_up, moe_w_down)
    return y_p.reshape(bp, sp, d), y_s.reshape(bs, ss, d)
```
